```python
import math
import jax
import jax.numpy as jnp
from jax import lax
import numpy as np

D_MODEL = 1024
BATCH = 4
SEQ = 8192
DEPTH = 2

CHUNK = 64
Q_BLOCK = 128
ROPE_THETA = 500000.0
HEAD_DIM = 64

MLA_HEADS = 8
DIFF_HEADS = 4
FOX_HEADS = 4
D_MIX = (MLA_HEADS + DIFF_HEADS + FOX_HEADS) * HEAD_DIM

MLA_Q_RANK = 384
MLA_KV_RANK = 256
MLA_NOPE = 64
MLA_ROPE = 32
MLA_V = HEAD_DIM

DIFF_QK = HEAD_DIM // 2
DIFF_V = HEAD_DIM
DIFF_ROT = DIFF_QK // 4
DIFF_W = DIFF_HEADS * HEAD_DIM

FOX_W = FOX_HEADS * HEAD_DIM

IN_SPLITS = (MLA_Q_RANK, MLA_KV_RANK, MLA_ROPE,
             DIFF_W, DIFF_W, DIFF_W,
             FOX_W, FOX_W, FOX_W, FOX_HEADS)
IN_COLS = MLA_Q_RANK + MLA_KV_RANK + MLA_ROPE + 3 * DIFF_W + 3 * FOX_W + FOX_HEADS

N_EXPERTS = 16
N_GROUPS = 4
GROUP_SIZE = N_EXPERTS // N_GROUPS
TOP_K = 2
D_EXPERT = 512

DEEPNORM_ALPHA = (2 * DEPTH) ** 0.25
DEEPNORM_BETA = (8 * DEPTH) ** -0.25

LN_EPS = 1e-5
RMS_EPS = 1e-6
NEG_INF = -1e30
F32 = jnp.float32

kernel_name = "hybrid_mla_diff_fox_grouped_moe_deepnorm"


def layer_norm(x, g, b):
    xf = x.astype(F32)
    mu = jnp.mean(xf, axis=-1, keepdims=True)
    var = jnp.mean(jnp.square(xf - mu), axis=-1, keepdims=True)
    return ((xf - mu) * lax.rsqrt(var + LN_EPS) * g.astype(F32) + b.astype(F32)).astype(x.dtype)


def rms_norm(x, g):
    xf = x.astype(F32)
    return (xf * lax.rsqrt(jnp.mean(xf * xf, axis=-1, keepdims=True) + RMS_EPS) * g.astype(F32)).astype(x.dtype)


def rope_tables(positions, rot_dim):
    inv_freq = ROPE_THETA ** (-jnp.arange(0, rot_dim, 2, dtype=F32) / rot_dim)
    ang = positions.astype(F32)[..., None] * inv_freq
    return jnp.cos(ang), jnp.sin(ang)


def apply_rope(x, cos, sin):
    xf = x.astype(F32)
    x1, x2 = jnp.split(xf, 2, axis=-1)
    c = cos[:, :, None, :]
    s = sin[:, :, None, :]
    return jnp.concatenate([x1 * c - x2 * s, x2 * c + x1 * s], axis=-1).astype(x.dtype)


def partial_rope(x, cos, sin, rot_dim):
    return jnp.concatenate([apply_rope(x[..., :rot_dim], cos, sin), x[..., rot_dim:]], axis=-1)


def q_block(a, i, axis=1):
    return lax.dynamic_slice_in_dim(a, i * Q_BLOCK, Q_BLOCK, axis)


def chunk_causal_mask(i, seq):
    q_idx = i * Q_BLOCK + jnp.arange(Q_BLOCK)
    k_idx = jnp.arange(seq)
    return (k_idx[None, :] // CHUNK) <= (q_idx[:, None] // CHUNK)


def frame_causal_mask(i, seq):
    q_idx = i * Q_BLOCK + jnp.arange(Q_BLOCK)
    k_idx = jnp.arange(seq)
    return k_idx[None, :] <= q_idx[:, None]


def sweep_query_blocks(block_fn, seq):
    out = lax.map(block_fn, jnp.arange(seq // Q_BLOCK))
    out = jnp.moveaxis(out, 0, 1)
    return out.reshape(out.shape[0], seq, out.shape[3], out.shape[4])


def mla_attention(c_q, c_kv, k_rope_raw, q_norm_g, kv_norm_g, w_uq, w_ukv, cos, sin):
    b, seq, _ = c_q.shape
    q = (rms_norm(c_q, q_norm_g) @ w_uq).reshape(b, seq, MLA_HEADS, MLA_NOPE + MLA_ROPE)
    q_nope = q[..., :MLA_NOPE]
    q_rope = apply_rope(q[..., MLA_NOPE:], cos, sin)
    kv = (rms_norm(c_kv, kv_norm_g) @ w_ukv).reshape(b, seq, MLA_HEADS, MLA_NOPE + MLA_V)
    k_nope = kv[..., :MLA_NOPE]
    v = kv[..., MLA_NOPE:]
    k_rope = apply_rope(k_rope_raw[:, :, None, :], cos, sin)[:, :, 0, :]
    scale = (MLA_NOPE + MLA_ROPE) ** -0.5

    def block(i):
        sc = (jnp.einsum('bqhd,bkhd->bhqk', q_block(q_nope, i), k_nope, preferred_element_type=F32)
              + jnp.einsum('bqhr,bkr->bhqk', q_block(q_rope, i), k_rope, preferred_element_type=F32)) * scale
        sc = jnp.where(chunk_causal_mask(i, seq), sc, NEG_INF)
        p = jax.nn.softmax(sc, axis=-1).astype(v.dtype)
        return jnp.einsum('bhqk,bkhd->bqhd', p, v)

    return sweep_query_blocks(block, seq)


def diff_attention(q, k, v, lam_q1, lam_k1, lam_q2, lam_k2, subln_g, lam_init, cos, sin):
    b, seq, _ = q.shape
    q = partial_rope(q.reshape(b, seq, 2 * DIFF_HEADS, DIFF_QK), cos, sin, DIFF_ROT)
    k = partial_rope(k.reshape(b, seq, 2 * DIFF_HEADS, DIFF_QK), cos, sin, DIFF_ROT)
    q1, q2 = q[:, :, 0::2], q[:, :, 1::2]
    k1, k2 = k[:, :, 0::2], k[:, :, 1::2]
    v = v.reshape(b, seq, DIFF_HEADS, DIFF_V)
    lam = (jnp.exp(jnp.sum(lam_q1.astype(F32) * lam_k1.astype(F32)))
           - jnp.exp(jnp.sum(lam_q2.astype(F32) * lam_k2.astype(F32))) + lam_init)
    scale = DIFF_QK ** -0.5

    def block(i):
        mask = chunk_causal_mask(i, seq)
        s1 = jnp.einsum('bqhd,bkhd->bhqk', q_block(q1, i), k1, preferred_element_type=F32) * scale
        s2 = jnp.einsum('bqhd,bkhd->bhqk', q_block(q2, i), k2, preferred_element_type=F32) * scale
        p = (jax.nn.softmax(jnp.where(mask, s1, NEG_INF), axis=-1)
             - lam * jax.nn.softmax(jnp.where(mask, s2, NEG_INF), axis=-1)).astype(v.dtype)
        return jnp.einsum('bhqk,bkhd->bqhd', p, v)

    o = sweep_query_blocks(block, seq)
    return rms_norm(o, subln_g) * (1.0 - lam_init)


def forgetting_attention(q, k, v, f_logit, f_bias):
    b, seq, _ = q.shape
    q = q.reshape(b, seq, FOX_HEADS, HEAD_DIM)
    k = k.reshape(b, seq, FOX_HEADS, HEAD_DIM)
    v = v.reshape(b, seq, FOX_HEADS, HEAD_DIM)
    log_f = jax.nn.log_sigmoid(f_logit.astype(F32) + f_bias.astype(F32))
    cum = jnp.cumsum(log_f, axis=1).transpose(0, 2, 1)
    scale = HEAD_DIM ** -0.5

    def block(i):
        sc = jnp.einsum('bqhd,bkhd->bhqk', q_block(q, i), k, preferred_element_type=F32) * scale
        sc = sc + q_block(cum, i, axis=2)[..., :, None] - cum[..., None, :]
        sc = jnp.where(frame_causal_mask(i, seq), sc, NEG_INF)
        p = jax.nn.softmax(sc, axis=-1).astype(v.dtype)
        return jnp.einsum('bhqk,bkhd->bqhd', p, v)

    return sweep_query_blocks(block, seq)


def mixer_sublayer(h, cos_m, sin_m, cos_d, sin_d, w_in, mla_q_norm_g, mla_kv_norm_g, mla_w_uq, mla_w_ukv,
                   diff_lam_q1, diff_lam_k1, diff_lam_q2, diff_lam_k2, diff_subln_g, lam_init, fox_f_bias, w_out):
    b, seq, _ = h.shape
    proj = h @ w_in
    parts = []
    start = 0
    for width in IN_SPLITS:
        parts.append(proj[..., start:start + width])
        start += width
    c_q, c_kv, k_rope, dq, dk, dv, fq, fk, fv, ff = parts
    o_mla = mla_attention(c_q, c_kv, k_rope, mla_q_norm_g, mla_kv_norm_g, mla_w_uq, mla_w_ukv, cos_m, sin_m)
    o_diff = diff_attention(dq, dk, dv, diff_lam_q1, diff_lam_k1, diff_lam_q2, diff_lam_k2,
                            diff_subln_g, lam_init, cos_d, sin_d)
    o_fox = forgetting_attention(fq, fk, fv, ff, fox_f_bias)
    o = jnp.concatenate([o_mla.reshape(b, seq, -1), o_diff.reshape(b, seq, -1),
                         o_fox.reshape(b, seq, -1)], axis=-1)
    return o @ w_out


def grouped_moe(h, router_w, router_bias, w_gate, w_up, w_down):
    n = h.shape[0]
    scores = jax.nn.sigmoid(jnp.einsum('nd,de->ne', h, router_w, preferred_element_type=F32))
    biased = (scores + router_bias.astype(F32)).reshape(n, N_GROUPS, GROUP_SIZE)
    group_score = jnp.sum(lax.top_k(biased, TOP_K)[0], axis=-1)
    group_sel = jnp.argmax(group_score, axis=-1)
    in_group = group_sel[:, None] == jnp.arange(N_GROUPS)[None, :]
    cand = jnp.where(in_group[:, :, None], biased, NEG_INF).reshape(n, N_EXPERTS)
    _, idx = lax.top_k(cand, TOP_K)
    sel = jnp.take_along_axis(scores, idx, axis=-1)
    gates = sel / jnp.sum(sel, axis=-1, keepdims=True)
    dense_gates = jnp.einsum('nk,nke->ne', gates, jax.nn.one_hot(idx, N_EXPERTS, dtype=F32)).astype(h.dtype)
    y = jnp.zeros_like(h)
    for e in range(N_EXPERTS):
        a = jax.nn.silu(h @ w_gate[e]) * (h @ w_up[e])
        y = y + dense_gates[:, e:e + 1] * (a @ w_down[e])
    return y


def setup_inputs(seed: int = 0) -> dict:
    key = jax.random.key(seed)
    ks = jax.random.split(key, 24)

    def nrm(k, shape, scale):
        return jax.random.normal(k, shape, F32) * scale

    def gain(k, shape):
        return 1.0 + nrm(k, shape, 0.02)

    x = nrm(ks[0], (BATCH, SEQ, D_MODEL), 1.0)
    offsets = jax.random.randint(ks[1], (BATCH, 1), 0, 64, dtype=jnp.int32) * CHUNK
    positions = offsets + jnp.arange(SEQ, dtype=jnp.int32)[None, :]
    return {
        "x": x,
        "positions": positions,
        "ln_in_g": gain(ks[2], (D_MODEL,)),
        "ln_in_b": nrm(ks[3], (D_MODEL,), 0.02),
        "w_in": nrm(ks[4], (DEPTH, D_MODEL, IN_COLS), D_MODEL ** -0.5),
        "mla_q_norm_g": gain(ks[5], (DEPTH, MLA_Q_RANK)),
        "mla_kv_norm_g": gain(ks[6], (DEPTH, MLA_KV_RANK)),
        "mla_w_uq": nrm(ks[7], (DEPTH, MLA_Q_RANK, MLA_HEADS * (MLA_NOPE + MLA_ROPE)), MLA_Q_RANK ** -0.5),
        "mla_w_ukv": nrm(ks[8], (DEPTH, MLA_KV_RANK, MLA_HEADS * (MLA_NOPE + MLA_V)), MLA_KV_RANK ** -0.5),
        "diff_lam_q1": nrm(ks[9], (DEPTH, DIFF_QK), 0.1),
        "diff_lam_k1": nrm(ks[10], (DEPTH, DIFF_QK), 0.1),
        "diff_lam_q2": nrm(ks[11], (DEPTH, DIFF_QK), 0.1),
        "diff_lam_k2": nrm(ks[12], (DEPTH, DIFF_QK), 0.1),
        "diff_subln_g": gain(ks[13], (DEPTH, DIFF_V)),
        "fox_f_bias": jax.random.uniform(ks[14], (DEPTH, FOX_HEADS), F32, 1.0, 4.0),
        "w_out": nrm(ks[15], (DEPTH, D_MIX, D_MODEL), D_MIX ** -0.5 * DEEPNORM_BETA),
        "ln1_g": gain(ks[16], (DEPTH, D_MODEL)),
        "ln1_b": nrm(ks[17], (DEPTH, D_MODEL), 0.02),
        "router_w": nrm(ks[18], (D_MODEL, N_EXPERTS), D_MODEL ** -0.5),
        "router_bias": nrm(ks[19], (N_EXPERTS,), 0.01),
        "exp_w_gate": nrm(ks[20], (DEPTH, N_EXPERTS, D_MODEL, D_EXPERT), D_MODEL ** -0.5),
        "exp_w_up": nrm(ks[21], (DEPTH, N_EXPERTS, D_MODEL, D_EXPERT), D_MODEL ** -0.5),
        "exp_w_down": nrm(ks[22], (DEPTH, N_EXPERTS, D_EXPERT, D_MODEL), D_EXPERT ** -0.5 * DEEPNORM_BETA),
        "ln2_g": gain(ks[23], (DEPTH, D_MODEL)),
        "ln2_b": nrm(jax.random.fold_in(ks[23], 1), (DEPTH, D_MODEL), 0.02),
    }


def reference(x, positions, ln_in_g, ln_in_b, w_in, mla_q_norm_g, mla_kv_norm_g, mla_w_uq, mla_w_ukv,
              diff_lam_q1, diff_lam_k1, diff_lam_q2, diff_lam_k2, diff_subln_g, fox_f_bias, w_out,
              ln1_g, ln1_b, router_w, router_bias, exp_w_gate, exp_w_up, exp_w_down, ln2_g, ln2_b):
    b, seq, d = x.shape
    cos_m, sin_m = rope_tables(positions, MLA_ROPE)
    cos_d, sin_d = rope_tables(positions, DIFF_ROT)
    h = layer_norm(x, ln_in_g, ln_in_b)
    for l in range(DEPTH):
        lam_init = 0.8 - 0.6 * math.exp(-0.3 * l)
        mix = mixer_sublayer(h, cos_m, sin_m, cos_d, sin_d, w_in[l], mla_q_norm_g[l], mla_kv_norm_g[l],
                             mla_w_uq[l], mla_w_ukv[l], diff_lam_q1[l], diff_lam_k1[l], diff_lam_q2[l],
                             diff_lam_k2[l], diff_subln_g[l], lam_init, fox_f_bias[l], w_out[l])
        h = layer_norm(DEEPNORM_ALPHA * h + mix, ln1_g[l], ln1_b[l])
        ffn = grouped_moe(h.reshape(b * seq, d), router_w, router_bias,
                          exp_w_gate[l], exp_w_up[l], exp_w_down[l]).reshape(b, seq, d)
        h = layer_norm(DEEPNORM_ALPHA * h + ffn, ln2_g[l], ln2_b[l])
    return h
```

```python
import functools
import math

import numpy as np
import jax
import jax.numpy as jnp
from jax import lax
from jax.experimental import pallas as pl
from jax.experimental.pallas import tpu as pltpu

F32 = jnp.float32
BF16 = jnp.bfloat16
HIGHEST = lax.Precision.HIGHEST

D_MODEL = 1024
HEAD_DIM = 64
MLA_HEADS = 8
DIFF_HEADS = 4
FOX_HEADS = 4
MLA_Q_RANK = 384
MLA_KV_RANK = 256
MLA_NOPE = 64
MLA_ROPE = 32
DIFF_QK = 32
DIFF_ROT = 8
N_EXPERTS = 16
N_GROUPS = 4
GROUP_SIZE = 4
D_EXPERT = 512
ROPE_THETA = 500000.0
CHUNK = 64
LN_EPS = 1e-5
RMS_EPS = 1e-6
NEG_INF = -1e30
LOG2E = 1.4426950408889634

LANES = 128
SUBLANES = 8
VMEM_LIMIT_BYTES = 56 * 1024 * 1024

ROW_TILE = 256
ATTN_TILE = 256
MOE_TILE = 256
CUM_TILE = 512

_C_CQ, _C_CKV, _C_KR, _C_KRS = 0, 384, 640, 768
_C_DQ, _C_DQS, _C_DK, _C_DKS, _C_DV = 896, 1152, 1408, 1664, 1920
_C_FQ, _C_FK, _C_FV, _C_END = 2176, 2688, 3200, 3456


def _cparams(sem):
    return pltpu.CompilerParams(dimension_semantics=sem, vmem_limit_bytes=VMEM_LIMIT_BYTES)


def _layer_norm(x, g, b):
    mu = jnp.mean(x, axis=-1, keepdims=True)
    xc = x - mu
    var = jnp.mean(xc * xc, axis=-1, keepdims=True)
    return xc * lax.rsqrt(var + LN_EPS) * g + b


def _ln_kernel(x_ref, g_ref, b_ref, o_ref):
    o_ref[...] = _layer_norm(x_ref[...], g_ref[...], b_ref[...])


def _ln_call(x, g, b):
    n, d = x.shape
    row = pl.BlockSpec((ROW_TILE, d), lambda i: (i, 0))
    vec = pl.BlockSpec((1, d), lambda i: (0, 0))
    return pl.pallas_call(
        _ln_kernel, grid=(n // ROW_TILE,), in_specs=[row, vec, vec], out_specs=row,
        out_shape=jax.ShapeDtypeStruct((n, d), F32), compiler_params=_cparams(("parallel",)),
        name="ln_in")(x, g, b)


def _rope_kernel(pos_ref, f128_ref, f256_ref, c128_ref, s128_ref, c256_ref, s256_ref):
    p = pos_ref[...].astype(F32)
    a = p * f128_ref[...]
    c128_ref[...] = jnp.cos(a)
    s128_ref[...] = jnp.sin(a)
    a2 = p * f256_ref[...]
    c256_ref[...] = jnp.cos(a2)
    s256_ref[...] = jnp.sin(a2)


def _rope_call(pos, f128, f256):
    n = pos.shape[0]
    spec = lambda w: pl.BlockSpec((ROW_TILE, w), lambda i: (i, 0))
    vec = lambda w: pl.BlockSpec((1, w), lambda i: (0, 0))
    return pl.pallas_call(
        _rope_kernel, grid=(n // ROW_TILE,),
        in_specs=[spec(1), vec(128), vec(256)],
        out_specs=[spec(128), spec(128), spec(256), spec(256)],
        out_shape=[jax.ShapeDtypeStruct((n, 128), F32), jax.ShapeDtypeStruct((n, 128), F32),
                   jax.ShapeDtypeStruct((n, 256), F32), jax.ShapeDtypeStruct((n, 256), F32)],
        compiler_params=_cparams(("parallel",)), name="rope_tables")(pos, f128, f256)


def _fox_bias_kernel(h_ref, w_ref, b_ref, o_ref, carry_ref):
    @pl.when(pl.program_id(1) == 0)
    def _():
        carry_ref[...] = jnp.zeros_like(carry_ref)

    logit = jnp.dot(h_ref[...], w_ref[...], preferred_element_type=F32, precision=HIGHEST) + b_ref[...]
    lf = jnp.minimum(logit, 0.0) - jnp.log1p(jnp.exp(-jnp.abs(logit)))
    t = lf.shape[0]
    r = lax.broadcasted_iota(jnp.int32, (t, t), 0)
    c = lax.broadcasted_iota(jnp.int32, (t, t), 1)
    tri = (r >= c).astype(F32)
    cum = jnp.dot(tri, lf, preferred_element_type=F32, precision=HIGHEST) + carry_ref[...]
    carry_ref[...] = cum[t - 1:t, :]
    o_ref[...] = cum * (-LOG2E)


def _fox_bias_call(h, w_ff, b_ff, batch, seq):
    t = min(CUM_TILE, seq)
    nt = seq // t
    return pl.pallas_call(
        _fox_bias_kernel, grid=(batch, nt),
        in_specs=[pl.BlockSpec((t, D_MODEL), lambda b, j: (b * nt + j, 0)),
                  pl.BlockSpec((D_MODEL, LANES), lambda b, j: (0, 0)),
                  pl.BlockSpec((1, LANES), lambda b, j: (0, 0))],
        out_specs=pl.BlockSpec((t, LANES), lambda b, j: (b * nt + j, 0)),
        out_shape=jax.ShapeDtypeStruct((batch * seq, LANES), F32),
        scratch_shapes=[pltpu.VMEM((1, LANES), F32)],
        compiler_params=_cparams(("arbitrary", "arbitrary")), name="fox_bias")(h, w_ff, b_ff)


def _prep_kernel(q_scales, h_ref, cb_ref, c128_ref, s128_ref, c256_ref, s256_ref,
                 w1_ref, wuq_ref, wukv_ref, gq_ref, gkv_ref, e_ref, ones_ref,
                 mq_ref, mk_ref, mvt_ref, dq_ref, dk_ref, dvt_ref, fq_ref, fk_ref, fvt_ref):
    qs_mla, qs_diff, qs_fox = q_scales
    proj = jnp.dot(h_ref[...].astype(BF16), w1_ref[...], preferred_element_type=F32)
    cos_m = c128_ref[...]
    sin_m = s128_ref[...]

    def rms(x, g):
        return (x * lax.rsqrt(jnp.mean(x * x, axis=-1, keepdims=True) + RMS_EPS) * g).astype(BF16)

    qe = jnp.dot(rms(proj[:, _C_CQ:_C_CKV], gq_ref[...]), wuq_ref[...], preferred_element_type=F32)
    for h in range(MLA_HEADS):
        q = qe[:, 128 * h:128 * h + 128] * cos_m + qe[:, 1024 + 128 * h:1152 + 128 * h] * sin_m
        mq_ref[0, h] = (q * qs_mla).astype(BF16)
    kve = jnp.dot(rms(proj[:, _C_CKV:_C_KR], gkv_ref[...]), wukv_ref[...], preferred_element_type=F32)
    kr = proj[:, _C_KR:_C_KRS] * cos_m + proj[:, _C_KRS:_C_DQ] * sin_m
    for h in range(MLA_HEADS):
        mk_ref[0, h] = (kve[:, 128 * h:128 * h + 128] + kr).astype(BF16)
    for p in range(MLA_HEADS // 2):
        mvt_ref[0, p, 0] = kve[:, 1024 + 128 * p:1152 + 128 * p].T.astype(BF16)

    cos_d = c256_ref[...]
    sin_d = s256_ref[...]
    dq = (proj[:, _C_DQ:_C_DQS] * cos_d + proj[:, _C_DQS:_C_DK] * sin_d) * qs_diff
    dk = proj[:, _C_DK:_C_DKS] * cos_d + proj[:, _C_DKS:_C_DV] * sin_d
    for p in range(DIFF_HEADS // 2):
        dq_ref[0, p] = dq[:, 128 * p:128 * p + 128].astype(BF16)
        dk_ref[0, p] = dk[:, 128 * p:128 * p + 128].astype(BF16)
        dvt_ref[0, p, 0] = proj[:, _C_DV + 128 * p:_C_DV + 128 * p + 128].T.astype(BF16)

    fq = proj[:, _C_FQ:_C_FK] * qs_fox + ones_ref[...]
    cb = cb_ref[...]
    hi = cb.astype(BF16)
    r1 = cb - hi.astype(F32)
    mid = r1.astype(BF16)
    lo = (r1 - mid.astype(F32)).astype(BF16)
    fkb = jnp.dot(jnp.concatenate([hi, mid, lo], axis=1), e_ref[...], preferred_element_type=F32)
    fk = proj[:, _C_FK:_C_FV] + fkb
    for h in range(FOX_HEADS):
        fq_ref[0, h] = fq[:, 128 * h:128 * h + 128].astype(BF16)
        fk_ref[0, h] = fk[:, 128 * h:128 * h + 128].astype(BF16)
    for p in range(FOX_HEADS // 2):
        fvt_ref[0, p, 0] = proj[:, _C_FV + 128 * p:_C_FV + 128 * p + 128].T.astype(BF16)


def _prep_call(h, cb, tables, w1, wuq, wukv, gq, gkv, emat, ones, batch, seq, q_scales):
    tm = ROW_TILE
    nt = seq // tm
    c128, s128, c256, s256 = tables
    row = lambda w: pl.BlockSpec((tm, w), lambda i: (i, 0))
    full = lambda a: pl.BlockSpec(a.shape, lambda i: (0,) * a.ndim)
    head = lambda nh: pl.BlockSpec((1, nh, tm, 128), lambda i: (i // nt, 0, i % nt, 0))
    vt = lambda npair: pl.BlockSpec((1, npair, 1, 128, tm), lambda i: (i // nt, 0, i % nt, 0, 0))
    hshape = lambda nh: jax.ShapeDtypeStruct((batch, nh, seq, 128), BF16)
    vshape = lambda npair: jax.ShapeDtypeStruct((batch, npair, nt, 128, tm), BF16)
    return pl.pallas_call(
        functools.partial(_prep_kernel, q_scales), grid=(batch * nt,),
        in_specs=[row(D_MODEL), row(128), row(128), row(128), row(256), row(256),
                  full(w1), full(wuq), full(wukv), full(gq), full(gkv), full(emat), full(ones)],
        out_specs=[head(8), head(8), vt(4), head(2), head(2), vt(2), head(4), head(4), vt(2)],
        out_shape=[hshape(8), hshape(8), vshape(4), hshape(2), hshape(2), vshape(2),
                   hshape(4), hshape(4), vshape(2)],
        compiler_params=_cparams(("parallel",)), name="prep")(
            h, cb, c128, s128, c256, s256, w1, wuq, wukv, gq, gkv, emat, ones)


def _attn_kernel(kind, lam_init, *refs):
    if kind == "diff":
        lam_ref, q_ref, k_ref, vt_ref, g_ref, o_ref, m_ref, l_ref, acc_ref = refs
    else:
        q_ref, k_ref, vt_ref, o_ref, m_ref, l_ref, acc_ref = refs
    t = ATTN_TILE
    qi = pl.program_id(2)

    if kind == "diff":
        q2 = q_ref[0, 0]
        lane = lax.broadcasted_iota(jnp.int32, q2.shape, 1)
        qs = [jnp.where((lane >= DIFF_QK * m) & (lane < DIFF_QK * (m + 1)), q2, jnp.zeros_like(q2))
              for m in range(4)]
        k_of_map = (0, 0, 0, 0)
        head_of_map = (0, 0, 1, 1)
    else:
        qs = [q_ref[0, 0], q_ref[0, 1]]
        k_of_map = (0, 1)
        head_of_map = (0, 1)
    nmap = len(qs)

    m_ref[...] = jnp.full(m_ref.shape, NEG_INF, F32)
    l_ref[...] = jnp.zeros(l_ref.shape, F32)
    acc_ref[...] = jnp.zeros(acc_ref.shape, F32)

    kk = lax.broadcasted_iota(jnp.int32, (t, t), 0)
    qq = lax.broadcasted_iota(jnp.int32, (t, t), 1)
    if kind == "fox":
        diag_mask = kk <= qq
    else:
        diag_mask = (kk // CHUNK) <= (qq // CHUNK)

    def block(j, masked):
        start = pl.multiple_of(j * t, t)
        for m in range(nmap):
            kblk = k_ref[0, k_of_map[m], pl.ds(start, t), :]
            s = lax.dot_general(kblk, qs[m], (((1,), (1,)), ((), ())), preferred_element_type=F32)
            if masked:
                s = jnp.where(diag_mask, s, NEG_INF)
            m_old = m_ref[m]
            m_new = jnp.maximum(m_old, jnp.max(s, axis=0, keepdims=True))
            alpha = jnp.exp2(m_old - m_new)
            p = jnp.exp2(s - m_new)
            l_ref[m] = alpha * l_ref[m] + jnp.sum(p, axis=0, keepdims=True)
            m_ref[m] = m_new
            hd = head_of_map[m]
            vblk = vt_ref[0, 0, j, HEAD_DIM * hd:HEAD_DIM * (hd + 1), :]
            pv = jnp.dot(vblk, p.astype(BF16), preferred_element_type=F32)
            acc_ref[m] = alpha * acc_ref[m] + pv

    def off_diag(j, carry):
        block(j, False)
        return carry

    lax.fori_loop(0, qi, off_diag, 0)
    block(qi, True)

    if kind == "diff":
        lam = lam_ref[0]
        outs = []
        for hd in range(2):
            o = acc_ref[2 * hd] / l_ref[2 * hd] - lam * (acc_ref[2 * hd + 1] / l_ref[2 * hd + 1])
            o = o * lax.rsqrt(jnp.mean(o * o, axis=0, keepdims=True) + RMS_EPS)
            outs.append(o)
        ot = jnp.concatenate(outs, axis=0).T
        o_ref[...] = (ot * g_ref[...] * (1.0 - lam_init)).astype(BF16)
    else:
        ot = jnp.concatenate([acc_ref[0] / l_ref[0], acc_ref[1] / l_ref[1]], axis=0).T
        o_ref[...] = ot.astype(BF16)


def _attn_call(kind, q, k, vt, batch, seq, lam=None, lam_init=0.0, subln_g=None):
    t = ATTN_TILE
    nq = seq // t
    npair = vt.shape[1]
    nmap = 4 if kind == "diff" else 2
    hq = 1 if kind == "diff" else 2
    in_specs = [pl.BlockSpec((1, hq, t, 128), lambda b, p, i: (b, p, i, 0)),
                pl.BlockSpec((1, hq, seq, 128), lambda b, p, i: (b, p, 0, 0)),
                pl.BlockSpec((1, 1, nq, 128, t), lambda b, p, i: (b, p, 0, 0, 0))]
    args = [q, k, vt]
    if kind == "diff":
        in_specs = [pl.BlockSpec(memory_space=pltpu.SMEM)] + in_specs + [
            pl.BlockSpec((1, 128), lambda b, p, i: (0, 0))]
        args = [lam] + args + [subln_g]
    return pl.pallas_call(
        functools.partial(_attn_kernel, kind, lam_init), grid=(batch, npair, nq),
        in_specs=in_specs,
        out_specs=pl.BlockSpec((t, 128), lambda b, p, i: (b * nq + i, p)),
        out_shape=jax.ShapeDtypeStruct((batch * seq, 128 * npair), BF16),
        scratch_shapes=[pltpu.VMEM((nmap, 1, t), F32), pltpu.VMEM((nmap, 1, t), F32),
                        pltpu.VMEM((nmap, HEAD_DIM, t), F32)],
        compiler_params=_cparams(("parallel", "parallel", "arbitrary")),
        name="attn_" + kind)(*args)


def _out_kernel(alpha, om_ref, od_ref, of_ref, h_ref, wo_ref, g_ref, b_ref, rwt_ref, rb_ref,
                h1_ref, meta_ref, cnt_out_ref, cnt_ref, sel_ref):
    @pl.when(pl.program_id(0) == 0)
    def _():
        cnt_ref[...] = jnp.zeros_like(cnt_ref)

    mix = (jnp.dot(om_ref[...], wo_ref[0:512, :], preferred_element_type=F32)
           + jnp.dot(od_ref[...], wo_ref[512:768, :], preferred_element_type=F32)
           + jnp.dot(of_ref[...], wo_ref[768:1024, :], preferred_element_type=F32))
    h1 = _layer_norm(alpha * h_ref[...] + mix, g_ref[...], b_ref[...])
    h1_ref[...] = h1

    logits = lax.dot_general(rwt_ref[...], h1, (((1,), (1,)), ((), ())),
                             preferred_element_type=F32, precision=HIGHEST)
    score = jax.nn.sigmoid(logits)
    biased = score + rb_ref[...]
    b = [biased[e:e + 1, :] for e in range(N_EXPERTS)]
    sc = [score[e:e + 1, :] for e in range(N_EXPERTS)]

    gscore = []
    for g in range(N_GROUPS):
        x = b[GROUP_SIZE * g:GROUP_SIZE * (g + 1)]
        pair_sums = [x[i] + x[j] for i in range(GROUP_SIZE) for j in range(i + 1, GROUP_SIZE)]
        gscore.append(functools.reduce(jnp.maximum, pair_sums))
    sel = []
    for g in range(N_GROUPS):
        first_max = None
        for g2 in range(N_GROUPS):
            if g2 == g:
                continue
            c = (gscore[g] > gscore[g2]) if g2 < g else (gscore[g] >= gscore[g2])
            first_max = c if first_max is None else (first_max & c)
        x = b[GROUP_SIZE * g:GROUP_SIZE * (g + 1)]
        for j in range(GROUP_SIZE):
            ahead = jnp.zeros_like(x[j])
            for i in range(GROUP_SIZE):
                if i < j:
                    ahead = ahead + (x[i] >= x[j]).astype(F32)
                elif i > j:
                    ahead = ahead + (x[i] > x[j]).astype(F32)
            sel.append(first_max & (ahead < 2.0))

    zero = jnp.zeros_like(sc[0])
    denom = zero
    e_lo = jnp.full_like(zero, 99.0)
    e_hi = jnp.full_like(zero, -1.0)
    for e in range(N_EXPERTS):
        sel_ref[e:e + 1, :] = sel[e].astype(F32)
        denom = denom + jnp.where(sel[e], sc[e], 0.0)
        e_lo = jnp.minimum(e_lo, jnp.where(sel[e], float(e), 99.0))
        e_hi = jnp.maximum(e_hi, jnp.where(sel[e], float(e), -1.0))

    selm = sel_ref[...]
    rows = selm.shape[1]
    r = lax.broadcasted_iota(jnp.int32, (rows, rows), 0)
    c = lax.broadcasted_iota(jnp.int32, (rows, rows), 1)
    before = (r < c).astype(BF16)
    rank = jnp.dot(selm.astype(BF16), before, preferred_element_type=F32) + cnt_ref[:, 0:1]
    g_lo, g_hi, r_lo, r_hi = zero, zero, zero, zero
    for e in range(N_EXPERTS):
        gate = sc[e] / denom
        is_lo = e_lo == float(e)
        is_hi = e_hi == float(e)
        g_lo = g_lo + jnp.where(is_lo, gate, 0.0)
        g_hi = g_hi + jnp.where(is_hi, gate, 0.0)
        r_lo = r_lo + jnp.where(is_lo, rank[e:e + 1, :], 0.0)
        r_hi = r_hi + jnp.where(is_hi, rank[e:e + 1, :], 0.0)
    for i, v in enumerate((e_lo, e_hi, r_lo, r_hi, g_lo, g_hi, zero, zero)):
        meta_ref[i:i + 1, :] = v
    cnt_ref[...] = cnt_ref[...] + jnp.sum(selm, axis=1, keepdims=True)
    cnt_out_ref[...] = cnt_ref[...]


def _out_call(om, od, of, h, wo, g, b, rwt, rb, alpha):
    n = h.shape[0]
    tm = ROW_TILE
    row = lambda w: pl.BlockSpec((tm, w), lambda i: (i, 0))
    full = lambda a: pl.BlockSpec(a.shape, lambda i: (0,) * a.ndim)
    return pl.pallas_call(
        functools.partial(_out_kernel, alpha), grid=(n // tm,),
        in_specs=[row(512), row(256), row(256), row(D_MODEL), full(wo), full(g), full(b),
                  full(rwt), full(rb)],
        out_specs=[row(D_MODEL), pl.BlockSpec((SUBLANES, tm), lambda i: (0, i)),
                   pl.BlockSpec((N_EXPERTS, LANES), lambda i: (0, 0))],
        out_shape=[jax.ShapeDtypeStruct((n, D_MODEL), F32), jax.ShapeDtypeStruct((SUBLANES, n), F32),
                   jax.ShapeDtypeStruct((N_EXPERTS, LANES), F32)],
        scratch_shapes=[pltpu.VMEM((N_EXPERTS, LANES), F32), pltpu.VMEM((N_EXPERTS, tm), F32)],
        compiler_params=_cparams(("arbitrary",)), name="outproj_router")(
            om, od, of, h, wo, g, b, rwt, rb)


def _row_copy(src_ref, src_row, dst_ref, dst_row, sem):
    return pltpu.make_async_copy(src_ref.at[pl.ds(src_row, 1)], dst_ref.at[pl.ds(dst_row, 1)], sem)


def _dispatch_kernel(rows_ref, h_ref, xs_in_ref, xs_ref, sem):
    del xs_in_ref
    tm = h_ref.shape[0]

    def start(r, carry):
        for s in range(2):
            _row_copy(h_ref, r, xs_ref, rows_ref[0, s, r], sem).start()
        return carry

    def wait(r, carry):
        for s in range(2):
            _row_copy(h_ref, r, xs_ref, rows_ref[0, s, r], sem).wait()
        return carry

    lax.fori_loop(0, tm, start, 0)
    lax.fori_loop(0, tm, wait, 0)


def _dispatch_call(rows, h1, total_rows):
    n, d = h1.shape
    tm = ROW_TILE
    xs0 = jnp.zeros((total_rows, d), F32)
    return pl.pallas_call(
        _dispatch_kernel, grid=(n // tm,),
        in_specs=[pl.BlockSpec((1, 2, tm), lambda i: (i, 0, 0), memory_space=pltpu.SMEM),
                  pl.BlockSpec((tm, d), lambda i: (i, 0)),
                  pl.BlockSpec(memory_space=pl.ANY)],
        out_specs=pl.BlockSpec(memory_space=pl.ANY),
        out_shape=jax.ShapeDtypeStruct((total_rows, d), F32),
        scratch_shapes=[pltpu.SemaphoreType.DMA(())],
        input_output_aliases={2: 0},
        compiler_params=_cparams(("arbitrary",)), name="moe_dispatch")(rows, h1, xs0)


def _moe_kernel(te_ref, nu_ref, x_ref, wg_ref, wu_ref, wd_ref, y_ref):
    del te_ref
    i = pl.program_id(0)

    @pl.when(i < nu_ref[0])
    def _():
        x = x_ref[...].astype(BF16)
        gate = jnp.dot(x, wg_ref[0], preferred_element_type=F32)
        up = jnp.dot(x, wu_ref[0], preferred_element_type=F32)
        a = (gate * jax.nn.sigmoid(gate) * up).astype(BF16)
        y_ref[...] = jnp.dot(a, wd_ref[0], preferred_element_type=F32)

    @pl.when(i >= nu_ref[0])
    def _():
        y_ref[...] = jnp.zeros_like(y_ref)


def _moe_call(tile_expert, n_used, xs, wg, wu, wd):
    total_rows, d = xs.shape
    tm = MOE_TILE
    grid_spec = pltpu.PrefetchScalarGridSpec(
        num_scalar_prefetch=2, grid=(total_rows // tm,),
        in_specs=[pl.BlockSpec((tm, d), lambda i, te, nu: (i, 0)),
                  pl.BlockSpec((1, d, D_EXPERT), lambda i, te, nu: (te[i], 0, 0)),
                  pl.BlockSpec((1, d, D_EXPERT), lambda i, te, nu: (te[i], 0, 0)),
                  pl.BlockSpec((1, D_EXPERT, d), lambda i, te, nu: (te[i], 0, 0))],
        out_specs=pl.BlockSpec((tm, d), lambda i, te, nu: (i, 0)))
    return pl.pallas_call(
        _moe_kernel, grid_spec=grid_spec,
        out_shape=jax.ShapeDtypeStruct((total_rows, d), F32),
        compiler_params=_cparams(("arbitrary",)), name="moe_experts")(
            tile_expert, n_used, xs, wg, wu, wd)


def _combine_kernel(alpha, rows_ref, meta_ref, h1_ref, ys_ref, g_ref, b_ref, o_ref, ybuf_ref, sem):
    tm = h1_ref.shape[0]

    def start(r, carry):
        for s in range(2):
            _row_copy(ys_ref, rows_ref[0, s, r], ybuf_ref.at[s], r, sem).start()
        return carry

    def wait(r, carry):
        for s in range(2):
            _row_copy(ys_ref, rows_ref[0, s, r], ybuf_ref.at[s], r, sem).wait()
        return carry

    lax.fori_loop(0, tm, start, 0)
    rr = lax.broadcasted_iota(jnp.int32, (tm, tm), 0)
    cc = lax.broadcasted_iota(jnp.int32, (tm, tm), 1)
    eye = (rr == cc).astype(F32)
    gcol = lax.dot_general(eye, meta_ref[...], (((1,), (1,)), ((), ())),
                           preferred_element_type=F32, precision=HIGHEST)
    lax.fori_loop(0, tm, wait, 0)
    ffn = gcol[:, 4:5] * ybuf_ref[0] + gcol[:, 5:6] * ybuf_ref[1]
    o_ref[...] = _layer_norm(alpha * h1_ref[...] + ffn, g_ref[...], b_ref[...])


def _combine_call(rows, meta, h1, ys, g, b, alpha):
    n, d = h1.shape
    tm = ROW_TILE
    vec = pl.BlockSpec((1, d), lambda i: (0, 0))
    return pl.pallas_call(
        functools.partial(_combine_kernel, alpha), grid=(n // tm,),
        in_specs=[pl.BlockSpec((1, 2, tm), lambda i: (i, 0, 0), memory_space=pltpu.SMEM),
                  pl.BlockSpec((SUBLANES, tm), lambda i: (0, i)),
                  pl.BlockSpec((tm, d), lambda i: (i, 0)),
                  pl.BlockSpec(memory_space=pl.ANY), vec, vec],
        out_specs=pl.BlockSpec((tm, d), lambda i: (i, 0)),
        out_shape=jax.ShapeDtypeStruct((n, d), F32),
        scratch_shapes=[pltpu.VMEM((2, tm, d), F32), pltpu.SemaphoreType.DMA(())],
        compiler_params=_cparams(("arbitrary",)), name="moe_combine")(rows, meta, h1, ys, g, b)


def _widen_w_in(w_in):
    z = lambda c: jnp.zeros((D_MODEL, c), F32)
    c_q, c_kv = w_in[:, 0:384], w_in[:, 384:640]
    kr = w_in[:, 640:672]
    dq, dk, dv = w_in[:, 672:928], w_in[:, 928:1184], w_in[:, 1184:1440]
    fq, fk, fv = w_in[:, 1440:1696], w_in[:, 1696:1952], w_in[:, 1952:2208]
    kr_blk = jnp.concatenate([z(64), kr, z(32)], axis=1)
    kr_swp = jnp.concatenate([z(64), -kr[:, 16:32], kr[:, 0:16], z(32)], axis=1)

    def diff_swap(w):
        w4 = w.reshape(D_MODEL, 8, DIFF_QK)
        half = DIFF_ROT // 2
        sw = jnp.concatenate([-w4[:, :, half:DIFF_ROT], w4[:, :, 0:half],
                              jnp.zeros((D_MODEL, 8, DIFF_QK - DIFF_ROT), F32)], axis=2)
        return sw.reshape(D_MODEL, 256)

    def pad_heads(w):
        w4 = w.reshape(D_MODEL, FOX_HEADS, HEAD_DIM)
        return jnp.concatenate([w4, jnp.zeros_like(w4)], axis=2).reshape(D_MODEL, FOX_HEADS * 128)

    w1 = jnp.concatenate([c_q, c_kv, kr_blk, kr_swp, dq, diff_swap(dq), dk, diff_swap(dk), dv,
                          pad_heads(fq), pad_heads(fk), fv], axis=1)
    assert w1.shape[1] == _C_END
    w_ff = jnp.concatenate([w_in[:, 2208:2212], z(LANES - FOX_HEADS)], axis=1)
    return w1.astype(BF16), w_ff


def _widen_mla(w_uq, w_ukv):
    wq = w_uq.reshape(MLA_Q_RANK, MLA_HEADS, MLA_NOPE + MLA_ROPE)
    zq = jnp.zeros((MLA_Q_RANK, MLA_HEADS, 32), F32)
    half = MLA_ROPE // 2
    q_main = jnp.concatenate([wq, zq], axis=2)
    q_swap = jnp.concatenate([jnp.zeros((MLA_Q_RANK, MLA_HEADS, MLA_NOPE), F32),
                              -wq[:, :, MLA_NOPE + half:], wq[:, :, MLA_NOPE:MLA_NOPE + half], zq], axis=2)
    wuq = jnp.concatenate([q_main.reshape(MLA_Q_RANK, 1024), q_swap.reshape(MLA_Q_RANK, 1024)], axis=1)
    wkv = w_ukv.reshape(MLA_KV_RANK, MLA_HEADS, 128)
    k_part = jnp.concatenate([wkv[:, :, :MLA_NOPE], jnp.zeros((MLA_KV_RANK, MLA_HEADS, 64), F32)], axis=2)
    v_part = wkv[:, :, MLA_NOPE:]
    wukv = jnp.concatenate([k_part.reshape(MLA_KV_RANK, 1024), v_part.reshape(MLA_KV_RANK, 512)], axis=1)
    return wuq.astype(BF16), wukv.astype(BF16)


def _fox_lane_constants():
    ones = np.zeros((1, FOX_HEADS * 128), np.float32)
    emat = np.zeros((3 * LANES, FOX_HEADS * 128), np.float32)
    for h in range(FOX_HEADS):
        for part in range(3):
            ones[0, 128 * h + HEAD_DIM + part] = 1.0
            emat[LANES * part + h, 128 * h + HEAD_DIM + part] = 1.0
    return jnp.asarray(ones), jnp.asarray(emat, dtype=BF16)


def _rope_lane_freqs():
    inv_m = ROPE_THETA ** (-jnp.arange(0, MLA_ROPE, 2, dtype=F32) / MLA_ROPE)
    f128 = jnp.concatenate([jnp.zeros((MLA_NOPE,), F32), inv_m, inv_m, jnp.zeros((32,), F32)])
    inv_d = ROPE_THETA ** (-jnp.arange(0, DIFF_ROT, 2, dtype=F32) / DIFF_ROT)
    f32w = jnp.concatenate([inv_d, inv_d, jnp.zeros((DIFF_QK - DIFF_ROT,), F32)])
    return f128.reshape(1, 128), jnp.tile(f32w, 8).reshape(1, 256)


def kernel(x, positions, ln_in_g, ln_in_b, w_in, mla_q_norm_g, mla_kv_norm_g, mla_w_uq, mla_w_ukv,
           diff_lam_q1, diff_lam_k1, diff_lam_q2, diff_lam_k2, diff_subln_g, fox_f_bias, w_out,
           ln1_g, ln1_b, router_w, router_bias, exp_w_gate, exp_w_up, exp_w_down, ln2_g, ln2_b):
    batch, seq, d = x.shape
    depth = w_in.shape[0]
    n = batch * seq
    alpha = (2 * depth) ** 0.25
    total_rows = 2 * n + N_EXPERTS * MOE_TILE
    n_tiles = total_rows // MOE_TILE
    q_scales = ((MLA_NOPE + MLA_ROPE) ** -0.5 * LOG2E, DIFF_QK ** -0.5 * LOG2E, HEAD_DIM ** -0.5 * LOG2E)

    row = lambda v: v.reshape(1, -1).astype(F32)
    f128, f256 = _rope_lane_freqs()
    tables = _rope_call(positions.reshape(n, 1).astype(jnp.int32), f128, f256)
    ones_fox, emat = _fox_lane_constants()
    rwt = router_w.T.astype(F32)
    rb = router_bias.reshape(N_EXPERTS, 1).astype(F32)

    h = _ln_call(x.reshape(n, d), row(ln_in_g), row(ln_in_b))
    for l in range(depth):
        lam_init = 0.8 - 0.6 * math.exp(-0.3 * l)
        lam = (jnp.exp(jnp.sum(diff_lam_q1[l].astype(F32) * diff_lam_k1[l].astype(F32)))
               - jnp.exp(jnp.sum(diff_lam_q2[l].astype(F32) * diff_lam_k2[l].astype(F32))) + lam_init)
        w1, w_ff = _widen_w_in(w_in[l])
        wuq, wukv = _widen_mla(mla_w_uq[l], mla_w_ukv[l])
        b_ff = jnp.concatenate([fox_f_bias[l].astype(F32), jnp.zeros((LANES - FOX_HEADS,), F32)]).reshape(1, LANES)

        cb = _fox_bias_call(h, w_ff, b_ff, batch, seq)
        mq, mk, mvt, dq, dk, dvt, fq, fk, fvt = _prep_call(
            h, cb, tables, w1, wuq, wukv, row(mla_q_norm_g[l]), row(mla_kv_norm_g[l]),
            emat, ones_fox, batch, seq, q_scales)
        o_mla = _attn_call("mla", mq, mk, mvt, batch, seq)
        o_diff = _attn_call("diff", dq, dk, dvt, batch, seq, lam=lam.reshape(1).astype(F32),
                            lam_init=lam_init, subln_g=row(jnp.tile(diff_subln_g[l], 2)))
        o_fox = _attn_call("fox", fq, fk, fvt, batch, seq)

        h1, meta, cnt = _out_call(o_mla, o_diff, o_fox, h, w_out[l].astype(BF16), row(ln1_g[l]),
                                  row(ln1_b[l]), rwt, rb, alpha)

        counts = cnt[:, 0].astype(jnp.int32)
        padded = ((counts + MOE_TILE - 1) // MOE_TILE) * MOE_TILE
        ends = jnp.cumsum(padded)
        starts = ends - padded
        e_idx = meta[0:2].astype(jnp.int32)
        rows = jnp.take(starts, e_idx) + meta[2:4].astype(jnp.int32)
        rows = rows.reshape(2, n // ROW_TILE, ROW_TILE).transpose(1, 0, 2)
        tile_start = jnp.arange(n_tiles, dtype=jnp.int32) * MOE_TILE
        tile_expert = jnp.minimum(jnp.sum(tile_start[:, None] >= ends[None, :], axis=1),
                                  N_EXPERTS - 1).astype(jnp.int32)
        n_used = (ends[-1:] // MOE_TILE).astype(jnp.int32)

        xs = _dispatch_call(rows, h1, total_rows)
        ys = _moe_call(tile_expert, n_used, xs, exp_w_gate[l].astype(BF16), exp_w_up[l].astype(BF16),
                       exp_w_down[l].astype(BF16))
        h = _combine_call(rows, meta, h1, ys, row(ln2_g[l]), row(ln2_b[l]), alpha)
    return h.reshape(batch, seq, d)
```

```python
import functools
import math

import numpy as np
import jax
import jax.numpy as jnp
from jax import lax
from jax.experimental import pallas as pl
from jax.experimental.pallas import tpu as pltpu

F32 = jnp.float32
BF16 = jnp.bfloat16
HIGHEST = lax.Precision.HIGHEST

D_MODEL = 1024
HEAD_DIM = 64
MLA_HEADS = 8
DIFF_HEADS = 4
FOX_HEADS = 4
MLA_Q_RANK = 384
MLA_KV_RANK = 256
MLA_NOPE = 64
MLA_ROPE = 32
DIFF_QK = 32
DIFF_ROT = 8
N_EXPERTS = 16
N_GROUPS = 4
GROUP_SIZE = 4
D_EXPERT = 512
ROPE_THETA = 500000.0
CHUNK = 64
LN_EPS = 1e-5
RMS_EPS = 1e-6
NEG_INF = -1e30
LOG2E = 1.4426950408889634

LANES = 128
SUBLANES = 8
VMEM_LIMIT_BYTES = 56 * 1024 * 1024

ROW_TILE = 256
ATTN_TILE = 256
MOE_TILE = 256
CUM_TILE = 512
DMA_UNROLL = 8

_C_CQ, _C_CKV, _C_KR, _C_KRS = 0, 384, 640, 768
_C_DQ, _C_DQS, _C_DK, _C_DKS, _C_DV = 896, 1152, 1408, 1664, 1920
_C_FQ, _C_FK, _C_FV, _C_END = 2176, 2688, 3200, 3456


def _cparams(sem):
    return pltpu.CompilerParams(dimension_semantics=sem, vmem_limit_bytes=VMEM_LIMIT_BYTES)


def _layer_norm(x, g, b):
    mu = jnp.mean(x, axis=-1, keepdims=True)
    xc = x - mu
    var = jnp.mean(xc * xc, axis=-1, keepdims=True)
    return xc * lax.rsqrt(var + LN_EPS) * g + b


def _ln_kernel(x_ref, g_ref, b_ref, o_ref):
    o_ref[...] = _layer_norm(x_ref[...], g_ref[...], b_ref[...])


def _ln_call(x, g, b):
    n, d = x.shape
    row = pl.BlockSpec((ROW_TILE, d), lambda i: (i, 0))
    vec = pl.BlockSpec((1, d), lambda i: (0, 0))
    return pl.pallas_call(
        _ln_kernel, grid=(n // ROW_TILE,), in_specs=[row, vec, vec], out_specs=row,
        out_shape=jax.ShapeDtypeStruct((n, d), F32), compiler_params=_cparams(("parallel",)),
        name="ln_in")(x, g, b)


def _rope_kernel(pos_ref, f128_ref, f256_ref, c128_ref, s128_ref, c256_ref, s256_ref):
    p = pos_ref[...].astype(F32)
    a = p * f128_ref[...]
    c128_ref[...] = jnp.cos(a)
    s128_ref[...] = jnp.sin(a)
    a2 = p * f256_ref[...]
    c256_ref[...] = jnp.cos(a2)
    s256_ref[...] = jnp.sin(a2)


def _rope_call(pos, f128, f256):
    n = pos.shape[0]
    spec = lambda w: pl.BlockSpec((ROW_TILE, w), lambda i: (i, 0))
    vec = lambda w: pl.BlockSpec((1, w), lambda i: (0, 0))
    return pl.pallas_call(
        _rope_kernel, grid=(n // ROW_TILE,),
        in_specs=[spec(1), vec(128), vec(256)],
        out_specs=[spec(128), spec(128), spec(256), spec(256)],
        out_shape=[jax.ShapeDtypeStruct((n, 128), F32), jax.ShapeDtypeStruct((n, 128), F32),
                   jax.ShapeDtypeStruct((n, 256), F32), jax.ShapeDtypeStruct((n, 256), F32)],
        compiler_params=_cparams(("parallel",)), name="rope_tables")(pos, f128, f256)


def _fox_bias_kernel(h_ref, w_ref, b_ref, o_ref, carry_ref):
    @pl.when(pl.program_id(1) == 0)
    def _():
        carry_ref[...] = jnp.zeros_like(carry_ref)

    logit = jnp.dot(h_ref[...], w_ref[...], preferred_element_type=F32, precision=HIGHEST) + b_ref[...]
    lf = jnp.minimum(logit, 0.0) - jnp.log1p(jnp.exp(-jnp.abs(logit)))
    t = lf.shape[0]
    r = lax.broadcasted_iota(jnp.int32, (t, t), 0)
    c = lax.broadcasted_iota(jnp.int32, (t, t), 1)
    tri = (r >= c).astype(F32)
    cum = jnp.dot(tri, lf, preferred_element_type=F32, precision=HIGHEST) + carry_ref[...]
    carry_ref[...] = cum[t - 1:t, :]
    o_ref[...] = cum * (-LOG2E)


def _fox_bias_call(h, w_ff, b_ff, batch, seq):
    t = min(CUM_TILE, seq)
    nt = seq // t
    return pl.pallas_call(
        _fox_bias_kernel, grid=(batch, nt),
        in_specs=[pl.BlockSpec((t, D_MODEL), lambda b, j: (b * nt + j, 0)),
                  pl.BlockSpec((D_MODEL, LANES), lambda b, j: (0, 0)),
                  pl.BlockSpec((1, LANES), lambda b, j: (0, 0))],
        out_specs=pl.BlockSpec((t, LANES), lambda b, j: (b * nt + j, 0)),
        out_shape=jax.ShapeDtypeStruct((batch * seq, LANES), F32),
        scratch_shapes=[pltpu.VMEM((1, LANES), F32)],
        compiler_params=_cparams(("arbitrary", "arbitrary")), name="fox_bias")(h, w_ff, b_ff)


def _prep_kernel(q_scales, h_ref, cb_ref, c128_ref, s128_ref, c256_ref, s256_ref,
                 w1_ref, wuq_ref, wukv_ref, gq_ref, gkv_ref, e_ref, ones_ref,
                 mq_ref, mk_ref, mvt_ref, dq_ref, dk_ref, dvt_ref, fq_ref, fk_ref, fvt_ref):
    qs_mla, qs_diff, qs_fox = q_scales
    proj = jnp.dot(h_ref[...].astype(BF16), w1_ref[...], preferred_element_type=F32)
    cos_m = c128_ref[...]
    sin_m = s128_ref[...]

    def rms(x, g):
        return (x * lax.rsqrt(jnp.mean(x * x, axis=-1, keepdims=True) + RMS_EPS) * g).astype(BF16)

    qe = jnp.dot(rms(proj[:, _C_CQ:_C_CKV], gq_ref[...]), wuq_ref[...], preferred_element_type=F32)
    for h in range(MLA_HEADS):
        q = qe[:, 128 * h:128 * h + 128] * cos_m + qe[:, 1024 + 128 * h:1152 + 128 * h] * sin_m
        mq_ref[0, h] = (q * qs_mla).astype(BF16)
    kve = jnp.dot(rms(proj[:, _C_CKV:_C_KR], gkv_ref[...]), wukv_ref[...], preferred_element_type=F32)
    kr = proj[:, _C_KR:_C_KRS] * cos_m + proj[:, _C_KRS:_C_DQ] * sin_m
    for h in range(MLA_HEADS):
        mk_ref[0, h] = (kve[:, 128 * h:128 * h + 128] + kr).astype(BF16)
    for p in range(MLA_HEADS // 2):
        mvt_ref[0, p, 0] = kve[:, 1024 + 128 * p:1152 + 128 * p].T.astype(BF16)

    cos_d = c256_ref[...]
    sin_d = s256_ref[...]
    dq = (proj[:, _C_DQ:_C_DQS] * cos_d + proj[:, _C_DQS:_C_DK] * sin_d) * qs_diff
    dk = proj[:, _C_DK:_C_DKS] * cos_d + proj[:, _C_DKS:_C_DV] * sin_d
    for p in range(DIFF_HEADS // 2):
        dq_ref[0, p] = dq[:, 128 * p:128 * p + 128].astype(BF16)
        dk_ref[0, p] = dk[:, 128 * p:128 * p + 128].astype(BF16)
        dvt_ref[0, p, 0] = proj[:, _C_DV + 128 * p:_C_DV + 128 * p + 128].T.astype(BF16)

    fq = proj[:, _C_FQ:_C_FK] * qs_fox + ones_ref[...]
    cb = cb_ref[...]
    hi = cb.astype(BF16)
    r1 = cb - hi.astype(F32)
    mid = r1.astype(BF16)
    lo = (r1 - mid.astype(F32)).astype(BF16)
    fkb = jnp.dot(jnp.concatenate([hi, mid, lo], axis=1), e_ref[...], preferred_element_type=F32)
    fk = proj[:, _C_FK:_C_FV] + fkb
    for h in range(FOX_HEADS):
        fq_ref[0, h] = fq[:, 128 * h:128 * h + 128].astype(BF16)
        fk_ref[0, h] = fk[:, 128 * h:128 * h + 128].astype(BF16)
    for p in range(FOX_HEADS // 2):
        fvt_ref[0, p, 0] = proj[:, _C_FV + 128 * p:_C_FV + 128 * p + 128].T.astype(BF16)


def _prep_call(h, cb, tables, w1, wuq, wukv, gq, gkv, emat, ones, batch, seq, q_scales):
    tm = ROW_TILE
    nt = seq // tm
    c128, s128, c256, s256 = tables
    row = lambda w: pl.BlockSpec((tm, w), lambda i: (i, 0))
    full = lambda a: pl.BlockSpec(a.shape, lambda i: (0,) * a.ndim)
    head = lambda nh: pl.BlockSpec((1, nh, tm, 128), lambda i: (i // nt, 0, i % nt, 0))
    vt = lambda npair: pl.BlockSpec((1, npair, 1, 128, tm), lambda i: (i // nt, 0, i % nt, 0, 0))
    hshape = lambda nh: jax.ShapeDtypeStruct((batch, nh, seq, 128), BF16)
    vshape = lambda npair: jax.ShapeDtypeStruct((batch, npair, nt, 128, tm), BF16)
    return pl.pallas_call(
        functools.partial(_prep_kernel, q_scales), grid=(batch * nt,),
        in_specs=[row(D_MODEL), row(128), row(128), row(128), row(256), row(256),
                  full(w1), full(wuq), full(wukv), full(gq), full(gkv), full(emat), full(ones)],
        out_specs=[head(8), head(8), vt(4), head(2), head(2), vt(2), head(4), head(4), vt(2)],
        out_shape=[hshape(8), hshape(8), vshape(4), hshape(2), hshape(2), vshape(2),
                   hshape(4), hshape(4), vshape(2)],
        compiler_params=_cparams(("parallel",)), name="prep")(
            h, cb, c128, s128, c256, s256, w1, wuq, wukv, gq, gkv, emat, ones)


def _attn_kernel(kind, lam_init, *refs):
    if kind == "diff":
        lam_ref, q_ref, k_ref, vt_ref, g_ref, o_ref, m_ref, l_ref, acc_ref = refs
    else:
        q_ref, k_ref, vt_ref, o_ref, m_ref, l_ref, acc_ref = refs
    t = ATTN_TILE
    qi = pl.program_id(1)
    npair = vt_ref.shape[1]

    qs, k_of_map, v_of_map = [], [], []
    if kind == "diff":
        for p in range(npair):
            q2 = q_ref[0, p]
            lane = lax.broadcasted_iota(jnp.int32, q2.shape, 1)
            for sub in range(4):
                keep = (lane >= DIFF_QK * sub) & (lane < DIFF_QK * (sub + 1))
                qs.append(jnp.where(keep, q2, jnp.zeros_like(q2)))
                k_of_map.append(p)
                v_of_map.append((p, sub // 2))
    else:
        for h in range(2 * npair):
            qs.append(q_ref[0, h])
            k_of_map.append(h)
            v_of_map.append((h // 2, h % 2))
    nmap = len(qs)

    m_ref[...] = jnp.full(m_ref.shape, NEG_INF, F32)
    l_ref[...] = jnp.zeros(l_ref.shape, F32)
    acc_ref[...] = jnp.zeros(acc_ref.shape, F32)

    kk = lax.broadcasted_iota(jnp.int32, (t, t), 0)
    qq = lax.broadcasted_iota(jnp.int32, (t, t), 1)
    if kind == "fox":
        diag_mask = kk <= qq
    else:
        diag_mask = (kk // CHUNK) <= (qq // CHUNK)

    def block(j, masked):
        start = pl.multiple_of(j * t, t)
        scores = []
        for m in range(nmap):
            kblk = k_ref[0, k_of_map[m], pl.ds(start, t), :]
            s = lax.dot_general(kblk, qs[m], (((1,), (1,)), ((), ())), preferred_element_type=F32)
            scores.append(jnp.where(diag_mask, s, NEG_INF) if masked else s)
        probs, alphas = [], []
        for m in range(nmap):
            m_old = m_ref[m]
            m_new = jnp.maximum(m_old, jnp.max(scores[m], axis=0, keepdims=True))
            alpha = jnp.exp2(m_old - m_new)
            p = jnp.exp2(scores[m] - m_new)
            l_ref[m] = alpha * l_ref[m] + jnp.sum(p, axis=0, keepdims=True)
            m_ref[m] = m_new
            probs.append(p.astype(BF16))
            alphas.append(alpha)
        for m in range(nmap):
            pair, hd = v_of_map[m]
            vblk = vt_ref[0, pair, j, HEAD_DIM * hd:HEAD_DIM * (hd + 1), :]
            pv = jnp.dot(vblk, probs[m], preferred_element_type=F32)
            acc_ref[m] = alphas[m] * acc_ref[m] + pv

    def off_diag(j, carry):
        block(j, False)
        return carry

    lax.fori_loop(0, qi, off_diag, 0)
    block(qi, True)

    for p in range(npair):
        if kind == "diff":
            lam = lam_ref[0]
            outs = []
            for hd in range(2):
                a, b = 4 * p + 2 * hd, 4 * p + 2 * hd + 1
                o = acc_ref[a] / l_ref[a] - lam * (acc_ref[b] / l_ref[b])
                outs.append(o * lax.rsqrt(jnp.mean(o * o, axis=0, keepdims=True) + RMS_EPS))
            ot = jnp.concatenate(outs, axis=0).T
            ot = ot * g_ref[...] * (1.0 - lam_init)
        else:
            a, b = 2 * p, 2 * p + 1
            ot = jnp.concatenate([acc_ref[a] / l_ref[a], acc_ref[b] / l_ref[b]], axis=0).T
        o_ref[:, 128 * p:128 * (p + 1)] = ot.astype(BF16)


def _attn_call(kind, q, k, vt, batch, seq, lam=None, lam_init=0.0, subln_g=None):
    t = ATTN_TILE
    nq = seq // t
    npair = vt.shape[1]
    nslab = q.shape[1]
    nmap = 4 * npair if kind == "diff" else 2 * npair
    resident = pl.Buffered(1)
    in_specs = [pl.BlockSpec((1, nslab, t, 128), lambda b, i: (b, 0, i, 0)),
                pl.BlockSpec((1, nslab, seq, 128), lambda b, i: (b, 0, 0, 0), pipeline_mode=resident),
                pl.BlockSpec((1, npair, nq, 128, t), lambda b, i: (b, 0, 0, 0, 0), pipeline_mode=resident)]
    args = [q, k, vt]
    if kind == "diff":
        in_specs = [pl.BlockSpec(memory_space=pltpu.SMEM)] + in_specs + [
            pl.BlockSpec((1, 128), lambda b, i: (0, 0))]
        args = [lam] + args + [subln_g]
    return pl.pallas_call(
        functools.partial(_attn_kernel, kind, lam_init), grid=(batch, nq),
        in_specs=in_specs,
        out_specs=pl.BlockSpec((t, 128 * npair), lambda b, i: (b * nq + i, 0)),
        out_shape=jax.ShapeDtypeStruct((batch * seq, 128 * npair), BF16),
        scratch_shapes=[pltpu.VMEM((nmap, 1, t), F32), pltpu.VMEM((nmap, 1, t), F32),
                        pltpu.VMEM((nmap, HEAD_DIM, t), F32)],
        compiler_params=_cparams(("parallel", "arbitrary")),
        name="attn_" + kind)(*args)


def _out_kernel(alpha, om_ref, od_ref, of_ref, h_ref, wo_ref, g_ref, b_ref, rwt_ref, rb_ref,
                h1_ref, meta_ref, cnt_out_ref, cnt_ref, sel_ref):
    @pl.when(pl.program_id(0) == 0)
    def _():
        cnt_ref[...] = jnp.zeros_like(cnt_ref)

    mix = (jnp.dot(om_ref[...], wo_ref[0:512, :], preferred_element_type=F32)
           + jnp.dot(od_ref[...], wo_ref[512:768, :], preferred_element_type=F32)
           + jnp.dot(of_ref[...], wo_ref[768:1024, :], preferred_element_type=F32))
    h1 = _layer_norm(alpha * h_ref[...] + mix, g_ref[...], b_ref[...])
    h1_ref[...] = h1

    logits = lax.dot_general(rwt_ref[...], h1, (((1,), (1,)), ((), ())),
                             preferred_element_type=F32, precision=HIGHEST)
    score = jax.nn.sigmoid(logits)
    biased = score + rb_ref[...]
    b = [biased[e:e + 1, :] for e in range(N_EXPERTS)]
    sc = [score[e:e + 1, :] for e in range(N_EXPERTS)]

    gscore = []
    for g in range(N_GROUPS):
        x = b[GROUP_SIZE * g:GROUP_SIZE * (g + 1)]
        pair_sums = [x[i] + x[j] for i in range(GROUP_SIZE) for j in range(i + 1, GROUP_SIZE)]
        gscore.append(functools.reduce(jnp.maximum, pair_sums))
    sel = []
    for g in range(N_GROUPS):
        first_max = None
        for g2 in range(N_GROUPS):
            if g2 == g:
                continue
            c = (gscore[g] > gscore[g2]) if g2 < g else (gscore[g] >= gscore[g2])
            first_max = c if first_max is None else (first_max & c)
        x = b[GROUP_SIZE * g:GROUP_SIZE * (g + 1)]
        for j in range(GROUP_SIZE):
            ahead = jnp.zeros_like(x[j])
            for i in range(GROUP_SIZE):
                if i < j:
                    ahead = ahead + (x[i] >= x[j]).astype(F32)
                elif i > j:
                    ahead = ahead + (x[i] > x[j]).astype(F32)
            sel.append(first_max & (ahead < 2.0))

    zero = jnp.zeros_like(sc[0])
    denom = zero
    e_lo = jnp.full_like(zero, 99.0)
    e_hi = jnp.full_like(zero, -1.0)
    for e in range(N_EXPERTS):
        sel_ref[e:e + 1, :] = sel[e].astype(F32)
        denom = denom + jnp.where(sel[e], sc[e], 0.0)
        e_lo = jnp.minimum(e_lo, jnp.where(sel[e], float(e), 99.0))
        e_hi = jnp.maximum(e_hi, jnp.where(sel[e], float(e), -1.0))

    selm = sel_ref[...]
    rows = selm.shape[1]
    r = lax.broadcasted_iota(jnp.int32, (rows, rows), 0)
    c = lax.broadcasted_iota(jnp.int32, (rows, rows), 1)
    before = (r < c).astype(BF16)
    rank = jnp.dot(selm.astype(BF16), before, preferred_element_type=F32) + cnt_ref[:, 0:1]
    g_lo, g_hi, r_lo, r_hi = zero, zero, zero, zero
    for e in range(N_EXPERTS):
        gate = sc[e] / denom
        is_lo = e_lo == float(e)
        is_hi = e_hi == float(e)
        g_lo = g_lo + jnp.where(is_lo, gate, 0.0)
        g_hi = g_hi + jnp.where(is_hi, gate, 0.0)
        r_lo = r_lo + jnp.where(is_lo, rank[e:e + 1, :], 0.0)
        r_hi = r_hi + jnp.where(is_hi, rank[e:e + 1, :], 0.0)
    for i, v in enumerate((e_lo, e_hi, r_lo, r_hi, g_lo, g_hi, zero, zero)):
        meta_ref[i:i + 1, :] = v
    cnt_ref[...] = cnt_ref[...] + jnp.sum(selm, axis=1, keepdims=True)
    cnt_out_ref[...] = cnt_ref[...]


def _out_call(om, od, of, h, wo, g, b, rwt, rb, alpha):
    n = h.shape[0]
    tm = ROW_TILE
    row = lambda w: pl.BlockSpec((tm, w), lambda i: (i, 0))
    full = lambda a: pl.BlockSpec(a.shape, lambda i: (0,) * a.ndim)
    return pl.pallas_call(
        functools.partial(_out_kernel, alpha), grid=(n // tm,),
        in_specs=[row(512), row(256), row(256), row(D_MODEL), full(wo), full(g), full(b),
                  full(rwt), full(rb)],
        out_specs=[row(D_MODEL), pl.BlockSpec((SUBLANES, tm), lambda i: (0, i)),
                   pl.BlockSpec((N_EXPERTS, LANES), lambda i: (0, 0))],
        out_shape=[jax.ShapeDtypeStruct((n, D_MODEL), F32), jax.ShapeDtypeStruct((SUBLANES, n), F32),
                   jax.ShapeDtypeStruct((N_EXPERTS, LANES), F32)],
        scratch_shapes=[pltpu.VMEM((N_EXPERTS, LANES), F32), pltpu.VMEM((N_EXPERTS, tm), F32)],
        compiler_params=_cparams(("arbitrary",)), name="outproj_router")(
            om, od, of, h, wo, g, b, rwt, rb)


def _row_copy(src_ref, src_row, dst_ref, dst_row, sem):
    return pltpu.make_async_copy(src_ref.at[pl.ds(src_row, 1)], dst_ref.at[pl.ds(dst_row, 1)], sem)


def _dispatch_kernel(rows_ref, h_ref, xs_in_ref, xs_ref, sem):
    del xs_in_ref
    tm = h_ref.shape[0]

    def start(c, carry):
        for u in range(DMA_UNROLL):
            r = c * DMA_UNROLL + u
            for s in range(2):
                _row_copy(h_ref, r, xs_ref, rows_ref[0, s, r], sem).start(priority=s)
        return carry

    def wait(c, carry):
        for _ in range(2 * DMA_UNROLL):
            _row_copy(h_ref, 0, xs_ref, 0, sem).wait()
        return carry

    lax.fori_loop(0, tm // DMA_UNROLL, start, 0)
    lax.fori_loop(0, tm // DMA_UNROLL, wait, 0)


def _dispatch_call(rows, h1, total_rows):
    n, d = h1.shape
    tm = ROW_TILE
    xs0 = jnp.zeros((total_rows, d), F32)
    return pl.pallas_call(
        _dispatch_kernel, grid=(n // tm,),
        in_specs=[pl.BlockSpec((1, 2, tm), lambda i: (i, 0, 0), memory_space=pltpu.SMEM),
                  pl.BlockSpec((tm, d), lambda i: (i, 0)),
                  pl.BlockSpec(memory_space=pl.ANY)],
        out_specs=pl.BlockSpec(memory_space=pl.ANY),
        out_shape=jax.ShapeDtypeStruct((total_rows, d), F32),
        scratch_shapes=[pltpu.SemaphoreType.DMA(())],
        input_output_aliases={2: 0},
        compiler_params=_cparams(("arbitrary",)), name="moe_dispatch")(rows, h1, xs0)


def _moe_kernel(te_ref, nu_ref, x_ref, wg_ref, wu_ref, wd_ref, y_ref):
    del te_ref
    i = pl.program_id(0)

    @pl.when(i < nu_ref[0])
    def _():
        x = x_ref[...].astype(BF16)
        gate = jnp.dot(x, wg_ref[0], preferred_element_type=F32)
        up = jnp.dot(x, wu_ref[0], preferred_element_type=F32)
        a = (gate * jax.nn.sigmoid(gate) * up).astype(BF16)
        y_ref[...] = jnp.dot(a, wd_ref[0], preferred_element_type=F32)

    @pl.when(i >= nu_ref[0])
    def _():
        y_ref[...] = jnp.zeros_like(y_ref)


def _moe_call(tile_expert, n_used, xs, wg, wu, wd):
    total_rows, d = xs.shape
    tm = MOE_TILE
    grid_spec = pltpu.PrefetchScalarGridSpec(
        num_scalar_prefetch=2, grid=(total_rows // tm,),
        in_specs=[pl.BlockSpec((tm, d), lambda i, te, nu: (i, 0)),
                  pl.BlockSpec((1, d, D_EXPERT), lambda i, te, nu: (te[i], 0, 0)),
                  pl.BlockSpec((1, d, D_EXPERT), lambda i, te, nu: (te[i], 0, 0)),
                  pl.BlockSpec((1, D_EXPERT, d), lambda i, te, nu: (te[i], 0, 0))],
        out_specs=pl.BlockSpec((tm, d), lambda i, te, nu: (i, 0)))
    return pl.pallas_call(
        _moe_kernel, grid_spec=grid_spec,
        out_shape=jax.ShapeDtypeStruct((total_rows, d), F32),
        compiler_params=_cparams(("arbitrary",)), name="moe_experts")(
            tile_expert, n_used, xs, wg, wu, wd)


def _combine_kernel(alpha, rows_ref, meta_ref, h1_ref, ys_ref, g_ref, b_ref, o_ref, ybuf_ref, sem):
    tm = h1_ref.shape[0]

    def start(c, carry):
        for u in range(DMA_UNROLL):
            r = c * DMA_UNROLL + u
            for s in range(2):
                _row_copy(ys_ref, rows_ref[0, s, r], ybuf_ref.at[s], r, sem).start(priority=s)
        return carry

    def wait(c, carry):
        for _ in range(2 * DMA_UNROLL):
            _row_copy(ys_ref, 0, ybuf_ref.at[0], 0, sem).wait()
        return carry

    lax.fori_loop(0, tm // DMA_UNROLL, start, 0)
    rr = lax.broadcasted_iota(jnp.int32, (tm, tm), 0)
    cc = lax.broadcasted_iota(jnp.int32, (tm, tm), 1)
    eye = (rr == cc).astype(F32)
    gcol = lax.dot_general(eye, meta_ref[...], (((1,), (1,)), ((), ())),
                           preferred_element_type=F32, precision=HIGHEST)
    lax.fori_loop(0, tm // DMA_UNROLL, wait, 0)
    ffn = gcol[:, 4:5] * ybuf_ref[0] + gcol[:, 5:6] * ybuf_ref[1]
    o_ref[...] = _layer_norm(alpha * h1_ref[...] + ffn, g_ref[...], b_ref[...])


def _combine_call(rows, meta, h1, ys, g, b, alpha):
    n, d = h1.shape
    tm = ROW_TILE
    vec = pl.BlockSpec((1, d), lambda i: (0, 0))
    return pl.pallas_call(
        functools.partial(_combine_kernel, alpha), grid=(n // tm,),
        in_specs=[pl.BlockSpec((1, 2, tm), lambda i: (i, 0, 0), memory_space=pltpu.SMEM),
                  pl.BlockSpec((SUBLANES, tm), lambda i: (0, i)),
                  pl.BlockSpec((tm, d), lambda i: (i, 0)),
                  pl.BlockSpec(memory_space=pl.ANY), vec, vec],
        out_specs=pl.BlockSpec((tm, d), lambda i: (i, 0)),
        out_shape=jax.ShapeDtypeStruct((n, d), F32),
        scratch_shapes=[pltpu.VMEM((2, tm, d), F32), pltpu.SemaphoreType.DMA(())],
        compiler_params=_cparams(("arbitrary",)), name="moe_combine")(rows, meta, h1, ys, g, b)


def _widen_w_in(w_in):
    z = lambda c: jnp.zeros((D_MODEL, c), F32)
    c_q, c_kv = w_in[:, 0:384], w_in[:, 384:640]
    kr = w_in[:, 640:672]
    dq, dk, dv = w_in[:, 672:928], w_in[:, 928:1184], w_in[:, 1184:1440]
    fq, fk, fv = w_in[:, 1440:1696], w_in[:, 1696:1952], w_in[:, 1952:2208]
    kr_blk = jnp.concatenate([z(64), kr, z(32)], axis=1)
    kr_swp = jnp.concatenate([z(64), -kr[:, 16:32], kr[:, 0:16], z(32)], axis=1)

    def diff_swap(w):
        w4 = w.reshape(D_MODEL, 8, DIFF_QK)
        half = DIFF_ROT // 2
        sw = jnp.concatenate([-w4[:, :, half:DIFF_ROT], w4[:, :, 0:half],
                              jnp.zeros((D_MODEL, 8, DIFF_QK - DIFF_ROT), F32)], axis=2)
        return sw.reshape(D_MODEL, 256)

    def pad_heads(w):
        w4 = w.reshape(D_MODEL, FOX_HEADS, HEAD_DIM)
        return jnp.concatenate([w4, jnp.zeros_like(w4)], axis=2).reshape(D_MODEL, FOX_HEADS * 128)

    w1 = jnp.concatenate([c_q, c_kv, kr_blk, kr_swp, dq, diff_swap(dq), dk, diff_swap(dk), dv,
                          pad_heads(fq), pad_heads(fk), fv], axis=1)
    assert w1.shape[1] == _C_END
    w_ff = jnp.concatenate([w_in[:, 2208:2212], z(LANES - FOX_HEADS)], axis=1)
    return w1.astype(BF16), w_ff


def _widen_mla(w_uq, w_ukv):
    wq = w_uq.reshape(MLA_Q_RANK, MLA_HEADS, MLA_NOPE + MLA_ROPE)
    zq = jnp.zeros((MLA_Q_RANK, MLA_HEADS, 32), F32)
    half = MLA_ROPE // 2
    q_main = jnp.concatenate([wq, zq], axis=2)
    q_swap = jnp.concatenate([jnp.zeros((MLA_Q_RANK, MLA_HEADS, MLA_NOPE), F32),
                              -wq[:, :, MLA_NOPE + half:], wq[:, :, MLA_NOPE:MLA_NOPE + half], zq], axis=2)
    wuq = jnp.concatenate([q_main.reshape(MLA_Q_RANK, 1024), q_swap.reshape(MLA_Q_RANK, 1024)], axis=1)
    wkv = w_ukv.reshape(MLA_KV_RANK, MLA_HEADS, 128)
    k_part = jnp.concatenate([wkv[:, :, :MLA_NOPE], jnp.zeros((MLA_KV_RANK, MLA_HEADS, 64), F32)], axis=2)
    v_part = wkv[:, :, MLA_NOPE:]
    wukv = jnp.concatenate([k_part.reshape(MLA_KV_RANK, 1024), v_part.reshape(MLA_KV_RANK, 512)], axis=1)
    return wuq.astype(BF16), wukv.astype(BF16)


def _fox_lane_constants():
    ones = np.zeros((1, FOX_HEADS * 128), np.float32)
    emat = np.zeros((3 * LANES, FOX_HEADS * 128), np.float32)
    for h in range(FOX_HEADS):
        for part in range(3):
            ones[0, 128 * h + HEAD_DIM + part] = 1.0
            emat[LANES * part + h, 128 * h + HEAD_DIM + part] = 1.0
    return jnp.asarray(ones), jnp.asarray(emat, dtype=BF16)


def _rope_lane_freqs():
    inv_m = ROPE_THETA ** (-jnp.arange(0, MLA_ROPE, 2, dtype=F32) / MLA_ROPE)
    f128 = jnp.concatenate([jnp.zeros((MLA_NOPE,), F32), inv_m, inv_m, jnp.zeros((32,), F32)])
    inv_d = ROPE_THETA ** (-jnp.arange(0, DIFF_ROT, 2, dtype=F32) / DIFF_ROT)
    f32w = jnp.concatenate([inv_d, inv_d, jnp.zeros((DIFF_QK - DIFF_ROT,), F32)])
    return f128.reshape(1, 128), jnp.tile(f32w, 8).reshape(1, 256)


def kernel(x, positions, ln_in_g, ln_in_b, w_in, mla_q_norm_g, mla_kv_norm_g, mla_w_uq, mla_w_ukv,
           diff_lam_q1, diff_lam_k1, diff_lam_q2, diff_lam_k2, diff_subln_g, fox_f_bias, w_out,
           ln1_g, ln1_b, router_w, router_bias, exp_w_gate, exp_w_up, exp_w_down, ln2_g, ln2_b):
    batch, seq, d = x.shape
    depth = w_in.shape[0]
    n = batch * seq
    alpha = (2 * depth) ** 0.25
    total_rows = 2 * n + N_EXPERTS * MOE_TILE
    n_tiles = total_rows // MOE_TILE
    q_scales = ((MLA_NOPE + MLA_ROPE) ** -0.5 * LOG2E, DIFF_QK ** -0.5 * LOG2E, HEAD_DIM ** -0.5 * LOG2E)

    row = lambda v: v.reshape(1, -1).astype(F32)
    f128, f256 = _rope_lane_freqs()
    tables = _rope_call(positions.reshape(n, 1).astype(jnp.int32), f128, f256)
    ones_fox, emat = _fox_lane_constants()
    rwt = router_w.T.astype(F32)
    rb = router_bias.reshape(N_EXPERTS, 1).astype(F32)

    h = _ln_call(x.reshape(n, d), row(ln_in_g), row(ln_in_b))
    for l in range(depth):
        lam_init = 0.8 - 0.6 * math.exp(-0.3 * l)
        lam = (jnp.exp(jnp.sum(diff_lam_q1[l].astype(F32) * diff_lam_k1[l].astype(F32)))
               - jnp.exp(jnp.sum(diff_lam_q2[l].astype(F32) * diff_lam_k2[l].astype(F32))) + lam_init)
        w1, w_ff = _widen_w_in(w_in[l])
        wuq, wukv = _widen_mla(mla_w_uq[l], mla_w_ukv[l])
        b_ff = jnp.concatenate([fox_f_bias[l].astype(F32), jnp.zeros((LANES - FOX_HEADS,), F32)]).reshape(1, LANES)

        cb = _fox_bias_call(h, w_ff, b_ff, batch, seq)
        mq, mk, mvt, dq, dk, dvt, fq, fk, fvt = _prep_call(
            h, cb, tables, w1, wuq, wukv, row(mla_q_norm_g[l]), row(mla_kv_norm_g[l]),
            emat, ones_fox, batch, seq, q_scales)
        o_mla = _attn_call("mla", mq, mk, mvt, batch, seq)
        o_diff = _attn_call("diff", dq, dk, dvt, batch, seq, lam=lam.reshape(1).astype(F32),
                            lam_init=lam_init, subln_g=row(jnp.tile(diff_subln_g[l], 2)))
        o_fox = _attn_call("fox", fq, fk, fvt, batch, seq)

        h1, meta, cnt = _out_call(o_mla, o_diff, o_fox, h, w_out[l].astype(BF16), row(ln1_g[l]),
                                  row(ln1_b[l]), rwt, rb, alpha)

        counts = cnt[:, 0].astype(jnp.int32)
        padded = ((counts + MOE_TILE - 1) // MOE_TILE) * MOE_TILE
        ends = jnp.cumsum(padded)
        starts = ends - padded
        e_idx = meta[0:2].astype(jnp.int32)
        expert_ids = jnp.arange(N_EXPERTS, dtype=jnp.int32)[:, None, None]
        rows = jnp.sum(jnp.where(e_idx[None] == expert_ids, starts[:, None, None], 0), axis=0)
        rows = rows + meta[2:4].astype(jnp.int32)
        rows = rows.reshape(2, n // ROW_TILE, ROW_TILE).transpose(1, 0, 2)
        tile_start = jnp.arange(n_tiles, dtype=jnp.int32) * MOE_TILE
        tile_expert = jnp.minimum(jnp.sum(tile_start[:, None] >= ends[None, :], axis=1),
                                  N_EXPERTS - 1).astype(jnp.int32)
        n_used = (ends[-1:] // MOE_TILE).astype(jnp.int32)

        xs = _dispatch_call(rows, h1, total_rows)
        ys = _moe_call(tile_expert, n_used, xs, exp_w_gate[l].astype(BF16), exp_w_up[l].astype(BF16),
                       exp_w_down[l].astype(BF16))
        h = _combine_call(rows, meta, h1, ys, row(ln2_g[l]), row(ln2_b[l]), alpha)
    return h.reshape(batch, seq, d)
```

```python
import functools
import math

import numpy as np
import jax
import jax.numpy as jnp
from jax import lax
from jax.experimental import pallas as pl
from jax.experimental.pallas import tpu as pltpu

F32 = jnp.float32
BF16 = jnp.bfloat16
HIGHEST = lax.Precision.HIGHEST

D_MODEL = 1024
HEAD_DIM = 64
MLA_HEADS = 8
DIFF_HEADS = 4
FOX_HEADS = 4
MLA_Q_RANK = 384
MLA_KV_RANK = 256
MLA_NOPE = 64
MLA_ROPE = 32
DIFF_QK = 32
DIFF_ROT = 8
N_EXPERTS = 16
N_GROUPS = 4
GROUP_SIZE = 4
D_EXPERT = 512
ROPE_THETA = 500000.0
CHUNK = 64
LN_EPS = 1e-5
RMS_EPS = 1e-6
NEG_INF = -1e30
LOG2E = 1.4426950408889634

LANES = 128
SUBLANES = 8
VMEM_LIMIT_BYTES = 56 * 1024 * 1024

ROW_TILE = 256
ATTN_TILE = 256
MOE_TILE = 512
KV_TILES_PER_STEP = {"mla": 2, "diff": 2, "fox": 4}
SUM_ROWS = 16
DMA_UNROLL = 8

_C_CQ, _C_CKV, _C_KR, _C_KRS = 0, 384, 640, 768
_C_DQ, _C_DQS, _C_DK, _C_DKS, _C_DV = 896, 1152, 1408, 1664, 1920
_C_FQ, _C_FK, _C_FV, _C_FF, _C_END = 2176, 2688, 3200, 3456, 3584


def _cparams(sem):
    return pltpu.CompilerParams(dimension_semantics=sem, vmem_limit_bytes=VMEM_LIMIT_BYTES)


def _split_bf16(x):
    hi = x.astype(BF16)
    return hi, (x - hi.astype(F32)).astype(BF16)


def _layer_norm(x, g, b):
    mu = jnp.mean(x, axis=-1, keepdims=True)
    xc = x - mu
    var = jnp.mean(xc * xc, axis=-1, keepdims=True)
    return xc * lax.rsqrt(var + LN_EPS) * g + b


def _ln_kernel(x_ref, g_ref, b_ref, o_ref):
    o_ref[...] = _layer_norm(x_ref[...], g_ref[...], b_ref[...])


def _ln_call(x, g, b):
    n, d = x.shape
    row = pl.BlockSpec((ROW_TILE, d), lambda i: (i, 0))
    vec = pl.BlockSpec((1, d), lambda i: (0, 0))
    return pl.pallas_call(
        _ln_kernel, grid=(n // ROW_TILE,), in_specs=[row, vec, vec], out_specs=row,
        out_shape=jax.ShapeDtypeStruct((n, d), F32), compiler_params=_cparams(("parallel",)),
        name="ln_in")(x, g, b)


def _rope_kernel(pos_ref, f128_ref, f256_ref, c128_ref, s128_ref, c256_ref, s256_ref):
    p = pos_ref[...].astype(F32)
    a = p * f128_ref[...]
    c128_ref[...] = jnp.cos(a)
    s128_ref[...] = jnp.sin(a)
    a2 = p * f256_ref[...]
    c256_ref[...] = jnp.cos(a2)
    s256_ref[...] = jnp.sin(a2)


def _rope_call(pos, f128, f256):
    n = pos.shape[0]
    spec = lambda w: pl.BlockSpec((ROW_TILE, w), lambda i: (i, 0))
    vec = lambda w: pl.BlockSpec((1, w), lambda i: (0, 0))
    return pl.pallas_call(
        _rope_kernel, grid=(n // ROW_TILE,),
        in_specs=[spec(1), vec(128), vec(256)],
        out_specs=[spec(128), spec(128), spec(256), spec(256)],
        out_shape=[jax.ShapeDtypeStruct((n, 128), F32), jax.ShapeDtypeStruct((n, 128), F32),
                   jax.ShapeDtypeStruct((n, 256), F32), jax.ShapeDtypeStruct((n, 256), F32)],
        compiler_params=_cparams(("parallel",)), name="rope_tables")(pos, f128, f256)


def _split3_bf16(x):
    hi = x.astype(BF16)
    r1 = x - hi.astype(F32)
    mid = r1.astype(BF16)
    lo = (r1 - mid.astype(F32)).astype(BF16)
    return jnp.concatenate([hi, mid, lo], axis=1)


def _prep_kernel(q_scales, tiles_per_seq, h_ref, c128_ref, s128_ref, c256_ref, s256_ref,
                 w1_ref, wuq_ref, wukv_ref, gq_ref, gkv_ref, e_ref, ones_ref, bff_ref,
                 mq_ref, mk_ref, mvt_ref, dq_ref, dk_ref, dvt_ref, fq_ref, fk_ref, fvt_ref, carry_ref):
    qs_mla, qs_diff, qs_fox = q_scales
    proj = jnp.dot(h_ref[...].astype(BF16), w1_ref[...], preferred_element_type=F32)
    cos_m = c128_ref[...]
    sin_m = s128_ref[...]

    def rms(x, g):
        return (x * lax.rsqrt(jnp.mean(x * x, axis=-1, keepdims=True) + RMS_EPS) * g).astype(BF16)

    qe = jnp.dot(rms(proj[:, _C_CQ:_C_CKV], gq_ref[...]), wuq_ref[...], preferred_element_type=F32)
    for h in range(MLA_HEADS):
        q = qe[:, 128 * h:128 * h + 128] * cos_m + qe[:, 1024 + 128 * h:1152 + 128 * h] * sin_m
        mq_ref[0, h] = (q * qs_mla).astype(BF16)
    kve = jnp.dot(rms(proj[:, _C_CKV:_C_KR], gkv_ref[...]), wukv_ref[...], preferred_element_type=F32)
    kr = proj[:, _C_KR:_C_KRS] * cos_m + proj[:, _C_KRS:_C_DQ] * sin_m
    for h in range(MLA_HEADS):
        mk_ref[0, h] = (kve[:, 128 * h:128 * h + 128] + kr).astype(BF16)
    for p in range(MLA_HEADS // 2):
        mvt_ref[0, p, 0] = kve[:, 1024 + 128 * p:1152 + 128 * p].T.astype(BF16)

    cos_d = c256_ref[...]
    sin_d = s256_ref[...]
    dq = (proj[:, _C_DQ:_C_DQS] * cos_d + proj[:, _C_DQS:_C_DK] * sin_d) * qs_diff
    dk = proj[:, _C_DK:_C_DKS] * cos_d + proj[:, _C_DKS:_C_DV] * sin_d
    for p in range(DIFF_HEADS // 2):
        dq_ref[0, p] = dq[:, 128 * p:128 * p + 128].astype(BF16)
        dk_ref[0, p] = dk[:, 128 * p:128 * p + 128].astype(BF16)
        dvt_ref[0, p, 0] = proj[:, _C_DV + 128 * p:_C_DV + 128 * p + 128].T.astype(BF16)

    fq = proj[:, _C_FQ:_C_FK] * qs_fox + ones_ref[...]

    @pl.when(pl.program_id(0) % tiles_per_seq == 0)
    def _():
        carry_ref[...] = jnp.zeros_like(carry_ref)

    logit = proj[:, _C_FF:_C_END] + bff_ref[...]
    log_f = jnp.minimum(logit, 0.0) - jnp.log1p(jnp.exp(-jnp.abs(logit)))
    rows = log_f.shape[0]
    tri = (lax.broadcasted_iota(jnp.int32, (rows, rows), 0)
           >= lax.broadcasted_iota(jnp.int32, (rows, rows), 1)).astype(BF16)
    c3 = jnp.dot(tri, _split3_bf16(log_f), preferred_element_type=F32)
    cum = c3[:, 0:LANES] + c3[:, LANES:2 * LANES] + c3[:, 2 * LANES:3 * LANES] + carry_ref[...]
    carry_ref[...] = cum[rows - 1:rows, :]
    fkb = jnp.dot(_split3_bf16(cum * (-LOG2E)), e_ref[...], preferred_element_type=F32)
    fk = proj[:, _C_FK:_C_FV] + fkb
    for h in range(FOX_HEADS):
        fq_ref[0, h] = fq[:, 128 * h:128 * h + 128].astype(BF16)
        fk_ref[0, h] = fk[:, 128 * h:128 * h + 128].astype(BF16)
    for p in range(FOX_HEADS // 2):
        fvt_ref[0, p, 0] = proj[:, _C_FV + 128 * p:_C_FV + 128 * p + 128].T.astype(BF16)


def _prep_call(h, tables, w1, wuq, wukv, gq, gkv, emat, ones, b_ff, batch, seq, q_scales):
    tm = ROW_TILE
    nt = seq // tm
    c128, s128, c256, s256 = tables
    row = lambda w: pl.BlockSpec((tm, w), lambda i: (i, 0))
    full = lambda a: pl.BlockSpec(a.shape, lambda i: (0,) * a.ndim)
    head = lambda nh: pl.BlockSpec((1, nh, tm, 128), lambda i: (i // nt, 0, i % nt, 0))
    vt = lambda npair: pl.BlockSpec((1, npair, 1, 128, tm), lambda i: (i // nt, 0, i % nt, 0, 0))
    hshape = lambda nh: jax.ShapeDtypeStruct((batch, nh, seq, 128), BF16)
    vshape = lambda npair: jax.ShapeDtypeStruct((batch, npair, nt, 128, tm), BF16)
    return pl.pallas_call(
        functools.partial(_prep_kernel, q_scales, nt), grid=(batch * nt,),
        in_specs=[row(D_MODEL), row(128), row(128), row(256), row(256),
                  full(w1), full(wuq), full(wukv), full(gq), full(gkv), full(emat), full(ones), full(b_ff)],
        out_specs=[head(8), head(8), vt(4), head(2), head(2), vt(2), head(4), head(4), vt(2)],
        out_shape=[hshape(8), hshape(8), vshape(4), hshape(2), hshape(2), vshape(2),
                   hshape(4), hshape(4), vshape(2)],
        scratch_shapes=[pltpu.VMEM((1, LANES), F32)],
        compiler_params=_cparams(("arbitrary",)), name="prep")(
            h, c128, s128, c256, s256, w1, wuq, wukv, gq, gkv, emat, ones, b_ff)


def _attn_kernel(kind, lam_init, *refs):
    if kind == "diff":
        lam_ref, q_ref, k_ref, vt_ref, g_ref, o_ref, m_ref, acc_ref, s_ref = refs
    else:
        q_ref, k_ref, vt_ref, o_ref, m_ref, acc_ref, s_ref = refs
    t = ATTN_TILE
    qi = pl.program_id(1)
    npair = vt_ref.shape[1]

    qs, k_of_map, v_of_map = [], [], []
    if kind == "diff":
        for p in range(npair):
            q2 = q_ref[0, p]
            lane = lax.broadcasted_iota(jnp.int32, q2.shape, 1)
            for sub in range(4):
                keep = (lane >= DIFF_QK * sub) & (lane < DIFF_QK * (sub + 1))
                qs.append(jnp.where(keep, q2, jnp.zeros_like(q2)))
                k_of_map.append(p)
                v_of_map.append((p, sub // 2))
    else:
        for h in range(2 * npair):
            qs.append(q_ref[0, h])
            k_of_map.append(h)
            v_of_map.append((h // 2, h % 2))
    nmap = len(qs)

    m_ref[...] = jnp.full(m_ref.shape, NEG_INF, F32)
    acc_ref[...] = jnp.zeros(acc_ref.shape, F32)

    kk = lax.broadcasted_iota(jnp.int32, (t, t), 0)
    qq = lax.broadcasted_iota(jnp.int32, (t, t), 1)
    if kind == "fox":
        diag_mask = kk <= qq
    else:
        diag_mask = (kk // CHUNK) <= (qq // CHUNK)
    ones_rows = jnp.ones((SUM_ROWS, t), BF16)

    def score_block(j, slot):
        start = pl.multiple_of(j * t, t)
        for m in range(nmap):
            kblk = k_ref[0, k_of_map[m], pl.ds(start, t), :]
            s_ref[slot, m] = lax.dot_general(kblk, qs[m], (((1,), (1,)), ((), ())),
                                             preferred_element_type=F32)

    def softmax_block(j, slot, masked):
        probs, alphas = [], []
        for m in range(nmap):
            s = s_ref[slot, m]
            if masked:
                s = jnp.where(diag_mask, s, NEG_INF)
            m_old = m_ref[m]
            m_new = jnp.maximum(m_old, jnp.max(s, axis=0, keepdims=True))
            alphas.append(jnp.exp2(m_old - m_new))
            probs.append(jnp.exp2(s - m_new).astype(BF16))
            m_ref[m] = m_new
        for m in range(nmap):
            pair, hd = v_of_map[m]
            vblk = jnp.concatenate([vt_ref[0, pair, j, HEAD_DIM * hd:HEAD_DIM * (hd + 1), :], ones_rows],
                                   axis=0)
            pv = jnp.dot(vblk, probs[m], preferred_element_type=F32)
            acc_ref[m] = alphas[m] * acc_ref[m] + pv

    score_block(0, 0)
    unroll = KV_TILES_PER_STEP[kind]

    def tiles(jj, carry):
        j = unroll * jj
        for u in range(unroll):
            score_block(j + u + 1, (u + 1) % 2)
            softmax_block(j + u, u % 2, False)
        return carry

    lax.fori_loop(0, qi // unroll, tiles, 0)
    base = qi - qi % unroll
    for rem in range(unroll):
        @pl.when(qi % unroll == rem)
        def _():
            for u in range(rem):
                score_block(base + u + 1, (u + 1) % 2)
                softmax_block(base + u, u % 2, False)
            softmax_block(qi, rem % 2, True)

    def normalized(m):
        return acc_ref[m, 0:HEAD_DIM, :] / acc_ref[m, HEAD_DIM:HEAD_DIM + 1, :]

    for p in range(npair):
        if kind == "diff":
            lam = lam_ref[0]
            outs = []
            for hd in range(2):
                o = normalized(4 * p + 2 * hd) - lam * normalized(4 * p + 2 * hd + 1)
                outs.append(o * lax.rsqrt(jnp.mean(o * o, axis=0, keepdims=True) + RMS_EPS))
            ot = jnp.concatenate(outs, axis=0).T
            ot = ot * g_ref[...] * (1.0 - lam_init)
        else:
            ot = jnp.concatenate([normalized(2 * p), normalized(2 * p + 1)], axis=0).T
        o_ref[:, 128 * p:128 * (p + 1)] = ot.astype(BF16)


def _attn_call(kind, q, k, vt, batch, seq, lam=None, lam_init=0.0, subln_g=None):
    t = ATTN_TILE
    nq = seq // t
    npair = vt.shape[1]
    nslab = q.shape[1]
    nmap = 4 * npair if kind == "diff" else 2 * npair
    resident = pl.Buffered(1)
    in_specs = [pl.BlockSpec((1, nslab, t, 128), lambda b, i: (b, 0, i, 0)),
                pl.BlockSpec((1, nslab, seq, 128), lambda b, i: (b, 0, 0, 0), pipeline_mode=resident),
                pl.BlockSpec((1, npair, nq, 128, t), lambda b, i: (b, 0, 0, 0, 0), pipeline_mode=resident)]
    args = [q, k, vt]
    if kind == "diff":
        in_specs = [pl.BlockSpec(memory_space=pltpu.SMEM)] + in_specs + [
            pl.BlockSpec((1, 128), lambda b, i: (0, 0))]
        args = [lam] + args + [subln_g]
    return pl.pallas_call(
        functools.partial(_attn_kernel, kind, lam_init), grid=(batch, nq),
        in_specs=in_specs,
        out_specs=pl.BlockSpec((t, 128 * npair), lambda b, i: (b * nq + i, 0)),
        out_shape=jax.ShapeDtypeStruct((batch * seq, 128 * npair), BF16),
        scratch_shapes=[pltpu.VMEM((nmap, 1, t), F32),
                        pltpu.VMEM((nmap, HEAD_DIM + SUM_ROWS, t), F32),
                        pltpu.VMEM((2, nmap, t, t), F32)],
        compiler_params=_cparams(("parallel", "arbitrary")),
        name="attn_" + kind)(*args)


def _out_kernel(alpha, om_ref, od_ref, of_ref, h_ref, wo_ref, g_ref, b_ref, rhi_ref, rlo_ref, rb_ref,
                h1_ref, meta_ref, cnt_out_ref, cnt_ref, sel_ref):
    @pl.when(pl.program_id(0) == 0)
    def _():
        cnt_ref[...] = jnp.zeros_like(cnt_ref)

    mix = (jnp.dot(om_ref[...], wo_ref[0:512, :], preferred_element_type=F32)
           + jnp.dot(od_ref[...], wo_ref[512:768, :], preferred_element_type=F32)
           + jnp.dot(of_ref[...], wo_ref[768:1024, :], preferred_element_type=F32))
    h1 = _layer_norm(alpha * h_ref[...] + mix, g_ref[...], b_ref[...])
    h1_ref[...] = h1

    h_hi, h_lo = _split_bf16(h1)
    nt = lambda x, y: lax.dot_general(x, y, (((1,), (1,)), ((), ())), preferred_element_type=F32)
    logits = nt(rhi_ref[...], h_hi) + nt(rhi_ref[...], h_lo) + nt(rlo_ref[...], h_hi)
    score = jax.nn.sigmoid(logits)
    biased = score + rb_ref[...]
    b = [biased[e:e + 1, :] for e in range(N_EXPERTS)]
    sc = [score[e:e + 1, :] for e in range(N_EXPERTS)]

    gscore = []
    for g in range(N_GROUPS):
        x = b[GROUP_SIZE * g:GROUP_SIZE * (g + 1)]
        pair_sums = [x[i] + x[j] for i in range(GROUP_SIZE) for j in range(i + 1, GROUP_SIZE)]
        gscore.append(functools.reduce(jnp.maximum, pair_sums))
    sel = []
    for g in range(N_GROUPS):
        first_max = None
        for g2 in range(N_GROUPS):
            if g2 == g:
                continue
            c = (gscore[g] > gscore[g2]) if g2 < g else (gscore[g] >= gscore[g2])
            first_max = c if first_max is None else (first_max & c)
        x = b[GROUP_SIZE * g:GROUP_SIZE * (g + 1)]
        for j in range(GROUP_SIZE):
            ahead = jnp.zeros_like(x[j])
            for i in range(GROUP_SIZE):
                if i < j:
                    ahead = ahead + (x[i] >= x[j]).astype(F32)
                elif i > j:
                    ahead = ahead + (x[i] > x[j]).astype(F32)
            sel.append(first_max & (ahead < 2.0))

    zero = jnp.zeros_like(sc[0])
    denom = zero
    e_lo = jnp.full_like(zero, 99.0)
    e_hi = jnp.full_like(zero, -1.0)
    for e in range(N_EXPERTS):
        sel_ref[e:e + 1, :] = sel[e].astype(F32)
        denom = denom + jnp.where(sel[e], sc[e], 0.0)
        e_lo = jnp.minimum(e_lo, jnp.where(sel[e], float(e), 99.0))
        e_hi = jnp.maximum(e_hi, jnp.where(sel[e], float(e), -1.0))

    selm = sel_ref[...]
    rows = selm.shape[1]
    r = lax.broadcasted_iota(jnp.int32, (rows, rows), 0)
    c = lax.broadcasted_iota(jnp.int32, (rows, rows), 1)
    before = (r < c).astype(BF16)
    rank = jnp.dot(selm.astype(BF16), before, preferred_element_type=F32) + cnt_ref[:, 0:1]
    g_lo, g_hi, r_lo, r_hi = zero, zero, zero, zero
    for e in range(N_EXPERTS):
        gate = sc[e] / denom
        is_lo = e_lo == float(e)
        is_hi = e_hi == float(e)
        g_lo = g_lo + jnp.where(is_lo, gate, 0.0)
        g_hi = g_hi + jnp.where(is_hi, gate, 0.0)
        r_lo = r_lo + jnp.where(is_lo, rank[e:e + 1, :], 0.0)
        r_hi = r_hi + jnp.where(is_hi, rank[e:e + 1, :], 0.0)
    for i, v in enumerate((e_lo, e_hi, r_lo, r_hi, g_lo, g_hi, zero, zero)):
        meta_ref[i:i + 1, :] = v
    cnt_ref[...] = cnt_ref[...] + jnp.sum(selm, axis=1, keepdims=True)
    cnt_out_ref[...] = cnt_ref[...]


def _out_call(om, od, of, h, wo, g, b, rwt, rb, alpha):
    r_hi, r_lo = _split_bf16(rwt)
    n = h.shape[0]
    tm = ROW_TILE
    row = lambda w: pl.BlockSpec((tm, w), lambda i: (i, 0))
    full = lambda a: pl.BlockSpec(a.shape, lambda i: (0,) * a.ndim)
    return pl.pallas_call(
        functools.partial(_out_kernel, alpha), grid=(n // tm,),
        in_specs=[row(512), row(256), row(256), row(D_MODEL), full(wo), full(g), full(b),
                  full(r_hi), full(r_lo), full(rb)],
        out_specs=[row(D_MODEL), pl.BlockSpec((SUBLANES, tm), lambda i: (0, i)),
                   pl.BlockSpec((N_EXPERTS, LANES), lambda i: (0, 0))],
        out_shape=[jax.ShapeDtypeStruct((n, D_MODEL), F32), jax.ShapeDtypeStruct((SUBLANES, n), F32),
                   jax.ShapeDtypeStruct((N_EXPERTS, LANES), F32)],
        scratch_shapes=[pltpu.VMEM((N_EXPERTS, LANES), F32), pltpu.VMEM((N_EXPERTS, tm), F32)],
        compiler_params=_cparams(("arbitrary",)), name="outproj_router")(
            om, od, of, h, wo, g, b, r_hi, r_lo, rb)


def _row_copy(src_ref, src_row, dst_ref, dst_row, sem):
    return pltpu.make_async_copy(src_ref.at[pl.ds(src_row, 1)], dst_ref.at[pl.ds(dst_row, 1)], sem)


def _dispatch_kernel(fill_ref, rows_ref, h_ref, xs_ref, zero_ref, sem):
    tm = h_ref.shape[0]

    @pl.when(pl.program_id(0) == 0)
    def _():
        zero_ref[...] = jnp.zeros_like(zero_ref)
        tail = fill_ref[1, N_EXPERTS - 1]
        n_tail = (xs_ref.shape[0] - tail) // tm

        def tail_copy(c):
            return pltpu.make_async_copy(zero_ref, xs_ref.at[pl.ds(pl.multiple_of(tail + c * tm, tm), tm)], sem)

        def tail_fill(c, carry):
            tail_copy(c).start()
            return carry

        def tail_wait(c, carry):
            tail_copy(c).wait()
            return carry

        lax.fori_loop(0, n_tail, tail_fill, 0)
        lax.fori_loop(0, n_tail, tail_wait, 0)
        for e in range(N_EXPERTS):
            def fill(r, carry):
                _row_copy(zero_ref, 0, xs_ref, r, sem).start()
                return carry

            def fill_wait(r, carry):
                _row_copy(zero_ref, 0, xs_ref, r, sem).wait()
                return carry

            lax.fori_loop(fill_ref[0, e], fill_ref[1, e], fill, 0)
            lax.fori_loop(fill_ref[0, e], fill_ref[1, e], fill_wait, 0)

    def start(c, carry):
        for u in range(DMA_UNROLL):
            r = c * DMA_UNROLL + u
            for s in range(2):
                _row_copy(h_ref, r, xs_ref, rows_ref[0, s, r], sem).start(priority=s)
        return carry

    def wait(c, carry):
        for _ in range(2 * DMA_UNROLL):
            _row_copy(h_ref, 0, xs_ref, 0, sem).wait()
        return carry

    lax.fori_loop(0, tm // DMA_UNROLL, start, 0)
    lax.fori_loop(0, tm // DMA_UNROLL, wait, 0)


def _dispatch_call(fill, rows, h1, total_rows):
    n, d = h1.shape
    tm = ROW_TILE
    grid_spec = pltpu.PrefetchScalarGridSpec(
        num_scalar_prefetch=1, grid=(n // tm,),
        in_specs=[pl.BlockSpec((1, 2, tm), lambda i, f: (i, 0, 0), memory_space=pltpu.SMEM),
                  pl.BlockSpec((tm, d), lambda i, f: (i, 0))],
        out_specs=pl.BlockSpec(memory_space=pl.ANY),
        scratch_shapes=[pltpu.VMEM((tm, d), F32), pltpu.SemaphoreType.DMA(())])
    return pl.pallas_call(
        _dispatch_kernel, grid_spec=grid_spec,
        out_shape=jax.ShapeDtypeStruct((total_rows, d), F32),
        compiler_params=_cparams(("arbitrary",)), name="moe_dispatch")(fill, rows, h1)


def _moe_kernel(te_ref, nu_ref, x_ref, wg_ref, wu_ref, wd_ref, y_ref):
    del te_ref
    i = pl.program_id(0)

    @pl.when(i < nu_ref[0])
    def _():
        x = x_ref[...].astype(BF16)
        gate = jnp.dot(x, wg_ref[0], preferred_element_type=F32)
        up = jnp.dot(x, wu_ref[0], preferred_element_type=F32)
        a = (gate * jax.nn.sigmoid(gate) * up).astype(BF16)
        y_ref[...] = jnp.dot(a, wd_ref[0], preferred_element_type=F32)

    @pl.when(i >= nu_ref[0])
    def _():
        y_ref[...] = jnp.zeros_like(y_ref)


def _moe_call(tile_expert, n_used, xs, wg, wu, wd):
    total_rows, d = xs.shape
    tm = MOE_TILE
    grid_spec = pltpu.PrefetchScalarGridSpec(
        num_scalar_prefetch=2, grid=(total_rows // tm,),
        in_specs=[pl.BlockSpec((tm, d), lambda i, te, nu: (jnp.minimum(i, nu[0] - 1), 0)),
                  pl.BlockSpec((1, d, D_EXPERT), lambda i, te, nu: (te[i], 0, 0)),
                  pl.BlockSpec((1, d, D_EXPERT), lambda i, te, nu: (te[i], 0, 0)),
                  pl.BlockSpec((1, D_EXPERT, d), lambda i, te, nu: (te[i], 0, 0))],
        out_specs=pl.BlockSpec((tm, d), lambda i, te, nu: (i, 0)))
    return pl.pallas_call(
        _moe_kernel, grid_spec=grid_spec,
        out_shape=jax.ShapeDtypeStruct((total_rows, d), F32),
        compiler_params=_cparams(("arbitrary",)), name="moe_experts")(
            tile_expert, n_used, xs, wg, wu, wd)


def _combine_kernel(alpha, rows_ref, meta_ref, h1_ref, ys_ref, g_ref, b_ref, o_ref, ybuf_ref, sem):
    tm = h1_ref.shape[0]

    def start(c, carry):
        for u in range(DMA_UNROLL):
            r = c * DMA_UNROLL + u
            for s in range(2):
                _row_copy(ys_ref, rows_ref[0, s, r], ybuf_ref.at[s], r, sem).start(priority=s)
        return carry

    def wait(c, carry):
        for _ in range(2 * DMA_UNROLL):
            _row_copy(ys_ref, 0, ybuf_ref.at[0], 0, sem).wait()
        return carry

    lax.fori_loop(0, tm // DMA_UNROLL, start, 0)
    rr = lax.broadcasted_iota(jnp.int32, (tm, tm), 0)
    cc = lax.broadcasted_iota(jnp.int32, (tm, tm), 1)
    eye = (rr == cc).astype(F32)
    gcol = lax.dot_general(eye, meta_ref[...], (((1,), (1,)), ((), ())),
                           preferred_element_type=F32, precision=HIGHEST)
    lax.fori_loop(0, tm // DMA_UNROLL, wait, 0)
    ffn = gcol[:, 4:5] * ybuf_ref[0] + gcol[:, 5:6] * ybuf_ref[1]
    o_ref[...] = _layer_norm(alpha * h1_ref[...] + ffn, g_ref[...], b_ref[...])


def _combine_call(rows, meta, h1, ys, g, b, alpha):
    n, d = h1.shape
    tm = ROW_TILE
    vec = pl.BlockSpec((1, d), lambda i: (0, 0))
    return pl.pallas_call(
        functools.partial(_combine_kernel, alpha), grid=(n // tm,),
        in_specs=[pl.BlockSpec((1, 2, tm), lambda i: (i, 0, 0), memory_space=pltpu.SMEM),
                  pl.BlockSpec((SUBLANES, tm), lambda i: (0, i)),
                  pl.BlockSpec((tm, d), lambda i: (i, 0)),
                  pl.BlockSpec(memory_space=pl.ANY), vec, vec],
        out_specs=pl.BlockSpec((tm, d), lambda i: (i, 0)),
        out_shape=jax.ShapeDtypeStruct((n, d), F32),
        scratch_shapes=[pltpu.VMEM((2, tm, d), F32), pltpu.SemaphoreType.DMA(())],
        compiler_params=_cparams(("arbitrary",)), name="moe_combine")(rows, meta, h1, ys, g, b)


def _widen_w_in(w_in):
    z = lambda c: jnp.zeros((D_MODEL, c), F32)
    c_q, c_kv = w_in[:, 0:384], w_in[:, 384:640]
    kr = w_in[:, 640:672]
    dq, dk, dv = w_in[:, 672:928], w_in[:, 928:1184], w_in[:, 1184:1440]
    fq, fk, fv = w_in[:, 1440:1696], w_in[:, 1696:1952], w_in[:, 1952:2208]
    kr_blk = jnp.concatenate([z(64), kr, z(32)], axis=1)
    kr_swp = jnp.concatenate([z(64), -kr[:, 16:32], kr[:, 0:16], z(32)], axis=1)

    def diff_swap(w):
        w4 = w.reshape(D_MODEL, 8, DIFF_QK)
        half = DIFF_ROT // 2
        sw = jnp.concatenate([-w4[:, :, half:DIFF_ROT], w4[:, :, 0:half],
                              jnp.zeros((D_MODEL, 8, DIFF_QK - DIFF_ROT), F32)], axis=2)
        return sw.reshape(D_MODEL, 256)

    def pad_heads(w):
        w4 = w.reshape(D_MODEL, FOX_HEADS, HEAD_DIM)
        return jnp.concatenate([w4, jnp.zeros_like(w4)], axis=2).reshape(D_MODEL, FOX_HEADS * 128)

    w_ff = jnp.concatenate([w_in[:, 2208:2212], z(LANES - FOX_HEADS)], axis=1)
    w1 = jnp.concatenate([c_q, c_kv, kr_blk, kr_swp, dq, diff_swap(dq), dk, diff_swap(dk), dv,
                          pad_heads(fq), pad_heads(fk), fv, w_ff], axis=1)
    assert w1.shape[1] == _C_END
    return w1.astype(BF16)


def _widen_mla(w_uq, w_ukv):
    wq = w_uq.reshape(MLA_Q_RANK, MLA_HEADS, MLA_NOPE + MLA_ROPE)
    zq = jnp.zeros((MLA_Q_RANK, MLA_HEADS, 32), F32)
    half = MLA_ROPE // 2
    q_main = jnp.concatenate([wq, zq], axis=2)
    q_swap = jnp.concatenate([jnp.zeros((MLA_Q_RANK, MLA_HEADS, MLA_NOPE), F32),
                              -wq[:, :, MLA_NOPE + half:], wq[:, :, MLA_NOPE:MLA_NOPE + half], zq], axis=2)
    wuq = jnp.concatenate([q_main.reshape(MLA_Q_RANK, 1024), q_swap.reshape(MLA_Q_RANK, 1024)], axis=1)
    wkv = w_ukv.reshape(MLA_KV_RANK, MLA_HEADS, 128)
    k_part = jnp.concatenate([wkv[:, :, :MLA_NOPE], jnp.zeros((MLA_KV_RANK, MLA_HEADS, 64), F32)], axis=2)
    v_part = wkv[:, :, MLA_NOPE:]
    wukv = jnp.concatenate([k_part.reshape(MLA_KV_RANK, 1024), v_part.reshape(MLA_KV_RANK, 512)], axis=1)
    return wuq.astype(BF16), wukv.astype(BF16)


def _fox_lane_constants():
    ones = np.zeros((1, FOX_HEADS * 128), np.float32)
    emat = np.zeros((3 * LANES, FOX_HEADS * 128), np.float32)
    for h in range(FOX_HEADS):
        for part in range(3):
            ones[0, 128 * h + HEAD_DIM + part] = 1.0
            emat[LANES * part + h, 128 * h + HEAD_DIM + part] = 1.0
    return jnp.asarray(ones), jnp.asarray(emat, dtype=BF16)


def _rope_lane_freqs():
    inv_m = ROPE_THETA ** (-jnp.arange(0, MLA_ROPE, 2, dtype=F32) / MLA_ROPE)
    f128 = jnp.concatenate([jnp.zeros((MLA_NOPE,), F32), inv_m, inv_m, jnp.zeros((32,), F32)])
    inv_d = ROPE_THETA ** (-jnp.arange(0, DIFF_ROT, 2, dtype=F32) / DIFF_ROT)
    f32w = jnp.concatenate([inv_d, inv_d, jnp.zeros((DIFF_QK - DIFF_ROT,), F32)])
    return f128.reshape(1, 128), jnp.tile(f32w, 8).reshape(1, 256)


def kernel(x, positions, ln_in_g, ln_in_b, w_in, mla_q_norm_g, mla_kv_norm_g, mla_w_uq, mla_w_ukv,
           diff_lam_q1, diff_lam_k1, diff_lam_q2, diff_lam_k2, diff_subln_g, fox_f_bias, w_out,
           ln1_g, ln1_b, router_w, router_bias, exp_w_gate, exp_w_up, exp_w_down, ln2_g, ln2_b):
    batch, seq, d = x.shape
    depth = w_in.shape[0]
    n = batch * seq
    alpha = (2 * depth) ** 0.25
    total_rows = 2 * n + N_EXPERTS * MOE_TILE
    n_tiles = total_rows // MOE_TILE
    q_scales = ((MLA_NOPE + MLA_ROPE) ** -0.5 * LOG2E, DIFF_QK ** -0.5 * LOG2E, HEAD_DIM ** -0.5 * LOG2E)

    row = lambda v: v.reshape(1, -1).astype(F32)
    f128, f256 = _rope_lane_freqs()
    tables = _rope_call(positions.reshape(n, 1).astype(jnp.int32), f128, f256)
    ones_fox, emat = _fox_lane_constants()
    rwt = router_w.T.astype(F32)
    rb = router_bias.reshape(N_EXPERTS, 1).astype(F32)

    h = _ln_call(x.reshape(n, d), row(ln_in_g), row(ln_in_b))
    for l in range(depth):
        lam_init = 0.8 - 0.6 * math.exp(-0.3 * l)
        lam = (jnp.exp(jnp.sum(diff_lam_q1[l].astype(F32) * diff_lam_k1[l].astype(F32)))
               - jnp.exp(jnp.sum(diff_lam_q2[l].astype(F32) * diff_lam_k2[l].astype(F32))) + lam_init)
        w1 = _widen_w_in(w_in[l])
        wuq, wukv = _widen_mla(mla_w_uq[l], mla_w_ukv[l])
        b_ff = jnp.concatenate([fox_f_bias[l].astype(F32), jnp.zeros((LANES - FOX_HEADS,), F32)]).reshape(1, LANES)

        mq, mk, mvt, dq, dk, dvt, fq, fk, fvt = _prep_call(
            h, tables, w1, wuq, wukv, row(mla_q_norm_g[l]), row(mla_kv_norm_g[l]),
            emat, ones_fox, b_ff, batch, seq, q_scales)
        o_mla = _attn_call("mla", mq, mk, mvt, batch, seq)
        o_diff = _attn_call("diff", dq, dk, dvt, batch, seq, lam=lam.reshape(1).astype(F32),
                            lam_init=lam_init, subln_g=row(jnp.tile(diff_subln_g[l], 2)))
        o_fox = _attn_call("fox", fq, fk, fvt, batch, seq)

        h1, meta, cnt = _out_call(o_mla, o_diff, o_fox, h, w_out[l].astype(BF16), row(ln1_g[l]),
                                  row(ln1_b[l]), rwt, rb, alpha)

        counts = cnt[:, 0].astype(jnp.int32)
        padded = ((counts + MOE_TILE - 1) // MOE_TILE) * MOE_TILE
        ends = jnp.cumsum(padded)
        starts = ends - padded
        e_idx = meta[0:2].astype(jnp.int32)
        expert_ids = jnp.arange(N_EXPERTS, dtype=jnp.int32)[:, None, None]
        rows = jnp.sum(jnp.where(e_idx[None] == expert_ids, starts[:, None, None], 0), axis=0)
        rows = rows + meta[2:4].astype(jnp.int32)
        rows = rows.reshape(2, n // ROW_TILE, ROW_TILE).transpose(1, 0, 2)
        tile_start = jnp.arange(n_tiles, dtype=jnp.int32) * MOE_TILE
        tile_expert = jnp.minimum(jnp.sum(tile_start[:, None] >= ends[None, :], axis=1),
                                  N_EXPERTS - 1).astype(jnp.int32)
        n_used = (ends[-1:] // MOE_TILE).astype(jnp.int32)

        fill = jnp.stack([starts + counts, ends]).astype(jnp.int32)
        xs = _dispatch_call(fill, rows, h1, total_rows)
        ys = _moe_call(tile_expert, n_used, xs, exp_w_gate[l].astype(BF16), exp_w_up[l].astype(BF16),
                       exp_w_down[l].astype(BF16))
        h = _combine_call(rows, meta, h1, ys, row(ln2_g[l]), row(ln2_b[l]), alpha)
    return h.reshape(batch, seq, d)
```

```python
import functools
import math

import numpy as np
import jax
import jax.numpy as jnp
from jax import lax
from jax.experimental import pallas as pl
from jax.experimental.pallas import tpu as pltpu

F32 = jnp.float32
BF16 = jnp.bfloat16
HIGHEST = lax.Precision.HIGHEST

D_MODEL = 1024
HEAD_DIM = 64
MLA_HEADS = 8
DIFF_HEADS = 4
FOX_HEADS = 4
MLA_Q_RANK = 384
MLA_KV_RANK = 256
MLA_NOPE = 64
MLA_ROPE = 32
DIFF_QK = 32
DIFF_ROT = 8
N_EXPERTS = 16
N_GROUPS = 4
GROUP_SIZE = 4
D_EXPERT = 512
ROPE_THETA = 500000.0
CHUNK = 64
LN_EPS = 1e-5
RMS_EPS = 1e-6
NEG_INF = -1e30
LOG2E = 1.4426950408889634

LANES = 128
SUBLANES = 8
VMEM_LIMIT_BYTES = 56 * 1024 * 1024

ROW_TILE = 256
ATTN_TILE = 256
MOE_TILE = 512
KV_TILES_PER_STEP = 4
SUM_ROWS = 16
DMA_UNROLL = 8

_C_CQ, _C_CKV, _C_KR, _C_KRS = 0, 384, 640, 768
_C_DQ, _C_DQS, _C_DK, _C_DKS, _C_DV = 896, 1152, 1408, 1664, 1920
_C_FQ, _C_FK, _C_FV, _C_FF, _C_END = 2176, 2688, 3200, 3456, 3584


def _cparams(sem):
    return pltpu.CompilerParams(dimension_semantics=sem, vmem_limit_bytes=VMEM_LIMIT_BYTES)


def _split_bf16(x):
    hi = x.astype(BF16)
    return hi, (x - hi.astype(F32)).astype(BF16)


def _layer_norm(x, g, b):
    mu = jnp.mean(x, axis=-1, keepdims=True)
    xc = x - mu
    var = jnp.mean(xc * xc, axis=-1, keepdims=True)
    return xc * lax.rsqrt(var + LN_EPS) * g + b


def _ln_kernel(x_ref, g_ref, b_ref, o_ref):
    o_ref[...] = _layer_norm(x_ref[...], g_ref[...], b_ref[...])


def _ln_call(x, g, b):
    n, d = x.shape
    row = pl.BlockSpec((ROW_TILE, d), lambda i: (i, 0))
    vec = pl.BlockSpec((1, d), lambda i: (0, 0))
    return pl.pallas_call(
        _ln_kernel, grid=(n // ROW_TILE,), in_specs=[row, vec, vec], out_specs=row,
        out_shape=jax.ShapeDtypeStruct((n, d), F32), compiler_params=_cparams(("parallel",)),
        name="ln_in")(x, g, b)


def _rope_kernel(pos_ref, f128_ref, f256_ref, c128_ref, s128_ref, c256_ref, s256_ref):
    p = pos_ref[...].astype(F32)
    a = p * f128_ref[...]
    c128_ref[...] = jnp.cos(a)
    s128_ref[...] = jnp.sin(a)
    a2 = p * f256_ref[...]
    c256_ref[...] = jnp.cos(a2)
    s256_ref[...] = jnp.sin(a2)


def _rope_call(pos, f128, f256):
    n = pos.shape[0]
    spec = lambda w: pl.BlockSpec((ROW_TILE, w), lambda i: (i, 0))
    vec = lambda w: pl.BlockSpec((1, w), lambda i: (0, 0))
    return pl.pallas_call(
        _rope_kernel, grid=(n // ROW_TILE,),
        in_specs=[spec(1), vec(128), vec(256)],
        out_specs=[spec(128), spec(128), spec(256), spec(256)],
        out_shape=[jax.ShapeDtypeStruct((n, 128), F32), jax.ShapeDtypeStruct((n, 128), F32),
                   jax.ShapeDtypeStruct((n, 256), F32), jax.ShapeDtypeStruct((n, 256), F32)],
        compiler_params=_cparams(("parallel",)), name="rope_tables")(pos, f128, f256)


def _split3_bf16(x):
    hi = x.astype(BF16)
    r1 = x - hi.astype(F32)
    mid = r1.astype(BF16)
    lo = (r1 - mid.astype(F32)).astype(BF16)
    return jnp.concatenate([hi, mid, lo], axis=1)


def _prep_kernel(q_scales, tiles_per_seq, h_ref, c128_ref, s128_ref, c256_ref, s256_ref,
                 w1_ref, wuq_ref, wukv_ref, gq_ref, gkv_ref, e_ref, ones_ref, bff_ref,
                 mq_ref, mk_ref, mvt_ref, dq_ref, dk_ref, dvt_ref, fq_ref, fk_ref, fvt_ref, carry_ref):
    qs_mla, qs_diff, qs_fox = q_scales
    proj = jnp.dot(h_ref[...].astype(BF16), w1_ref[...], preferred_element_type=F32)
    cos_m = c128_ref[...]
    sin_m = s128_ref[...]

    def rms(x, g):
        return (x * lax.rsqrt(jnp.mean(x * x, axis=-1, keepdims=True) + RMS_EPS) * g).astype(BF16)

    qe = jnp.dot(rms(proj[:, _C_CQ:_C_CKV], gq_ref[...]), wuq_ref[...], preferred_element_type=F32)
    for h in range(MLA_HEADS):
        q = qe[:, 128 * h:128 * h + 128] * cos_m + qe[:, 1024 + 128 * h:1152 + 128 * h] * sin_m
        mq_ref[0, h] = (q * qs_mla).astype(BF16)
    kve = jnp.dot(rms(proj[:, _C_CKV:_C_KR], gkv_ref[...]), wukv_ref[...], preferred_element_type=F32)
    kr = proj[:, _C_KR:_C_KRS] * cos_m + proj[:, _C_KRS:_C_DQ] * sin_m
    for h in range(MLA_HEADS):
        mk_ref[0, h] = (kve[:, 128 * h:128 * h + 128] + kr).astype(BF16)
    for p in range(MLA_HEADS // 2):
        mvt_ref[0, p, 0] = kve[:, 1024 + 128 * p:1152 + 128 * p].T.astype(BF16)

    cos_d = c256_ref[...]
    sin_d = s256_ref[...]
    dq = (proj[:, _C_DQ:_C_DQS] * cos_d + proj[:, _C_DQS:_C_DK] * sin_d) * qs_diff
    dk = proj[:, _C_DK:_C_DKS] * cos_d + proj[:, _C_DKS:_C_DV] * sin_d
    for p in range(DIFF_HEADS // 2):
        dq_ref[0, p] = dq[:, 128 * p:128 * p + 128].astype(BF16)
        dk_ref[0, p] = dk[:, 128 * p:128 * p + 128].astype(BF16)
        dvt_ref[0, p, 0] = proj[:, _C_DV + 128 * p:_C_DV + 128 * p + 128].T.astype(BF16)

    fq = proj[:, _C_FQ:_C_FK] * qs_fox + ones_ref[...]

    @pl.when(pl.program_id(0) % tiles_per_seq == 0)
    def _():
        carry_ref[...] = jnp.zeros_like(carry_ref)

    logit = proj[:, _C_FF:_C_END] + bff_ref[...]
    log_f = jnp.minimum(logit, 0.0) - jnp.log1p(jnp.exp(-jnp.abs(logit)))
    rows = log_f.shape[0]
    tri = (lax.broadcasted_iota(jnp.int32, (rows, rows), 0)
           >= lax.broadcasted_iota(jnp.int32, (rows, rows), 1)).astype(BF16)
    c3 = jnp.dot(tri, _split3_bf16(log_f), preferred_element_type=F32)
    cum = c3[:, 0:LANES] + c3[:, LANES:2 * LANES] + c3[:, 2 * LANES:3 * LANES] + carry_ref[...]
    carry_ref[...] = cum[rows - 1:rows, :]
    fkb = jnp.dot(_split3_bf16(cum * (-LOG2E)), e_ref[...], preferred_element_type=F32)
    fk = proj[:, _C_FK:_C_FV] + fkb
    for h in range(FOX_HEADS):
        fq_ref[0, h] = fq[:, 128 * h:128 * h + 128].astype(BF16)
        fk_ref[0, h] = fk[:, 128 * h:128 * h + 128].astype(BF16)
    for p in range(FOX_HEADS // 2):
        fvt_ref[0, p, 0] = proj[:, _C_FV + 128 * p:_C_FV + 128 * p + 128].T.astype(BF16)


def _prep_call(h, tables, w1, wuq, wukv, gq, gkv, emat, ones, b_ff, batch, seq, q_scales):
    tm = ROW_TILE
    nt = seq // tm
    c128, s128, c256, s256 = tables
    row = lambda w: pl.BlockSpec((tm, w), lambda i: (i, 0))
    full = lambda a: pl.BlockSpec(a.shape, lambda i: (0,) * a.ndim)
    head = lambda nh: pl.BlockSpec((1, nh, tm, 128), lambda i: (i // nt, 0, i % nt, 0))
    vt = lambda npair: pl.BlockSpec((1, npair, 1, 128, tm), lambda i: (i // nt, 0, i % nt, 0, 0))
    hshape = lambda nh: jax.ShapeDtypeStruct((batch, nh, seq, 128), BF16)
    vshape = lambda npair: jax.ShapeDtypeStruct((batch, npair, nt, 128, tm), BF16)
    return pl.pallas_call(
        functools.partial(_prep_kernel, q_scales, nt), grid=(batch * nt,),
        in_specs=[row(D_MODEL), row(128), row(128), row(256), row(256),
                  full(w1), full(wuq), full(wukv), full(gq), full(gkv), full(emat), full(ones), full(b_ff)],
        out_specs=[head(8), head(8), vt(4), head(2), head(2), vt(2), head(4), head(4), vt(2)],
        out_shape=[hshape(8), hshape(8), vshape(4), hshape(2), hshape(2), vshape(2),
                   hshape(4), hshape(4), vshape(2)],
        scratch_shapes=[pltpu.VMEM((1, LANES), F32)],
        compiler_params=_cparams(("arbitrary",)), name="prep")(
            h, c128, s128, c256, s256, w1, wuq, wukv, gq, gkv, emat, ones, b_ff)


def _attn_kernel(kind, lam_init, *refs):
    if kind == "diff":
        lam_ref, q_ref, k_ref, vt_ref, g_ref, o_ref, m_ref, acc_ref, s_ref, bm_ref = refs
    else:
        q_ref, k_ref, vt_ref, o_ref, m_ref, acc_ref, s_ref, bm_ref = refs
    t = ATTN_TILE
    qi = pl.program_id(1)
    npair = vt_ref.shape[1]

    qs, k_of_map, v_of_map = [], [], []
    if kind == "diff":
        for p in range(npair):
            q2 = q_ref[0, p]
            lane = lax.broadcasted_iota(jnp.int32, q2.shape, 1)
            for sub in range(4):
                keep = (lane >= DIFF_QK * sub) & (lane < DIFF_QK * (sub + 1))
                qs.append(jnp.where(keep, q2, jnp.zeros_like(q2)))
                k_of_map.append(p)
                v_of_map.append((p, sub // 2))
    else:
        for h in range(2 * npair):
            qs.append(q_ref[0, h])
            k_of_map.append(h)
            v_of_map.append((h // 2, h % 2))
    nmap = len(qs)

    m_ref[...] = jnp.full(m_ref.shape, NEG_INF, F32)
    acc_ref[...] = jnp.zeros(acc_ref.shape, F32)

    kk = lax.broadcasted_iota(jnp.int32, (t, t), 0)
    qq = lax.broadcasted_iota(jnp.int32, (t, t), 1)
    if kind == "fox":
        diag_mask = kk <= qq
    else:
        diag_mask = (kk // CHUNK) <= (qq // CHUNK)
    ones_rows = jnp.ones((SUM_ROWS, t), BF16)

    def score_block(j, slot):
        start = pl.multiple_of(j * t, t)
        for m in range(nmap):
            kblk = k_ref[0, k_of_map[m], pl.ds(start, t), :]
            s = lax.dot_general(kblk, qs[m], (((1,), (1,)), ((), ())), preferred_element_type=F32)
            s_ref[slot, m] = s
            bm_ref[slot, m] = jnp.max(s, axis=0, keepdims=True)

    def softmax_block(j, slot, masked):
        probs, alphas = [], []
        for m in range(nmap):
            s = s_ref[slot, m]
            m_old = m_ref[m]
            if masked:
                s = jnp.where(diag_mask, s, NEG_INF)
                m_new = jnp.maximum(m_old, jnp.max(s, axis=0, keepdims=True))
            else:
                m_new = jnp.maximum(m_old, bm_ref[slot, m])
            alphas.append(jnp.exp2(m_old - m_new))
            probs.append(jnp.exp2(s - m_new).astype(BF16))
            m_ref[m] = m_new
        for m in range(nmap):
            pair, hd = v_of_map[m]
            vblk = jnp.concatenate([vt_ref[0, pair, j, HEAD_DIM * hd:HEAD_DIM * (hd + 1), :], ones_rows],
                                   axis=0)
            pv = jnp.dot(vblk, probs[m], preferred_element_type=F32)
            acc_ref[m] = alphas[m] * acc_ref[m] + pv

    score_block(0, 0)
    unroll = KV_TILES_PER_STEP

    def tiles(jj, carry):
        j = unroll * jj
        for u in range(unroll):
            score_block(j + u + 1, (u + 1) % 2)
            softmax_block(j + u, u % 2, False)
        return carry

    lax.fori_loop(0, qi // unroll, tiles, 0)
    base = qi - qi % unroll
    for rem in range(unroll):
        @pl.when(qi % unroll == rem)
        def _():
            for u in range(rem):
                score_block(base + u + 1, (u + 1) % 2)
                softmax_block(base + u, u % 2, False)
            softmax_block(qi, rem % 2, True)

    def normalized(m):
        return acc_ref[m, 0:HEAD_DIM, :] / acc_ref[m, HEAD_DIM:HEAD_DIM + 1, :]

    for p in range(npair):
        if kind == "diff":
            lam = lam_ref[0]
            outs = []
            for hd in range(2):
                o = normalized(4 * p + 2 * hd) - lam * normalized(4 * p + 2 * hd + 1)
                outs.append(o * lax.rsqrt(jnp.mean(o * o, axis=0, keepdims=True) + RMS_EPS))
            ot = jnp.concatenate(outs, axis=0).T
            ot = ot * g_ref[...] * (1.0 - lam_init)
        else:
            ot = jnp.concatenate([normalized(2 * p), normalized(2 * p + 1)], axis=0).T
        o_ref[:, 128 * p:128 * (p + 1)] = ot.astype(BF16)


def _attn_call(kind, q, k, vt, batch, seq, lam=None, lam_init=0.0, subln_g=None):
    t = ATTN_TILE
    nq = seq // t
    npair = vt.shape[1]
    nslab = q.shape[1]
    nmap = 4 * npair if kind == "diff" else 2 * npair
    resident = pl.Buffered(1)
    in_specs = [pl.BlockSpec((1, nslab, t, 128), lambda b, i: (b, 0, i, 0)),
                pl.BlockSpec((1, nslab, seq, 128), lambda b, i: (b, 0, 0, 0), pipeline_mode=resident),
                pl.BlockSpec((1, npair, nq, 128, t), lambda b, i: (b, 0, 0, 0, 0), pipeline_mode=resident)]
    args = [q, k, vt]
    if kind == "diff":
        in_specs = [pl.BlockSpec(memory_space=pltpu.SMEM)] + in_specs + [
            pl.BlockSpec((1, 128), lambda b, i: (0, 0))]
        args = [lam] + args + [subln_g]
    return pl.pallas_call(
        functools.partial(_attn_kernel, kind, lam_init), grid=(batch, nq),
        in_specs=in_specs,
        out_specs=pl.BlockSpec((t, 128 * npair), lambda b, i: (b * nq + i, 0)),
        out_shape=jax.ShapeDtypeStruct((batch * seq, 128 * npair), BF16),
        scratch_shapes=[pltpu.VMEM((nmap, 1, t), F32),
                        pltpu.VMEM((nmap, HEAD_DIM + SUM_ROWS, t), F32),
                        pltpu.VMEM((2, nmap, t, t), F32), pltpu.VMEM((2, nmap, 1, t), F32)],
        compiler_params=_cparams(("parallel", "arbitrary")),
        name="attn_" + kind)(*args)


def _out_kernel(alpha, om_ref, od_ref, of_ref, h_ref, wo_ref, g_ref, b_ref, rhi_ref, rlo_ref, rb_ref,
                h1_ref, meta_ref, cnt_out_ref, cnt_ref, sel_ref):
    @pl.when(pl.program_id(0) == 0)
    def _():
        cnt_ref[...] = jnp.zeros_like(cnt_ref)

    mix = (jnp.dot(om_ref[...], wo_ref[0:512, :], preferred_element_type=F32)
           + jnp.dot(od_ref[...], wo_ref[512:768, :], preferred_element_type=F32)
           + jnp.dot(of_ref[...], wo_ref[768:1024, :], preferred_element_type=F32))
    h1 = _layer_norm(alpha * h_ref[...] + mix, g_ref[...], b_ref[...])
    h1_ref[...] = h1

    h_hi, h_lo = _split_bf16(h1)
    nt = lambda x, y: lax.dot_general(x, y, (((1,), (1,)), ((), ())), preferred_element_type=F32)
    logits = nt(rhi_ref[...], h_hi) + nt(rhi_ref[...], h_lo) + nt(rlo_ref[...], h_hi)
    score = jax.nn.sigmoid(logits)
    biased = score + rb_ref[...]
    b = [biased[e:e + 1, :] for e in range(N_EXPERTS)]
    sc = [score[e:e + 1, :] for e in range(N_EXPERTS)]

    gscore = []
    for g in range(N_GROUPS):
        x = b[GROUP_SIZE * g:GROUP_SIZE * (g + 1)]
        pair_sums = [x[i] + x[j] for i in range(GROUP_SIZE) for j in range(i + 1, GROUP_SIZE)]
        gscore.append(functools.reduce(jnp.maximum, pair_sums))
    sel = []
    for g in range(N_GROUPS):
        first_max = None
        for g2 in range(N_GROUPS):
            if g2 == g:
                continue
            c = (gscore[g] > gscore[g2]) if g2 < g else (gscore[g] >= gscore[g2])
            first_max = c if first_max is None else (first_max & c)
        x = b[GROUP_SIZE * g:GROUP_SIZE * (g + 1)]
        for j in range(GROUP_SIZE):
            ahead = jnp.zeros_like(x[j])
            for i in range(GROUP_SIZE):
                if i < j:
                    ahead = ahead + (x[i] >= x[j]).astype(F32)
                elif i > j:
                    ahead = ahead + (x[i] > x[j]).astype(F32)
            sel.append(first_max & (ahead < 2.0))

    zero = jnp.zeros_like(sc[0])
    denom = zero
    e_lo = jnp.full_like(zero, 99.0)
    e_hi = jnp.full_like(zero, -1.0)
    for e in range(N_EXPERTS):
        sel_ref[e:e + 1, :] = sel[e].astype(F32)
        denom = denom + jnp.where(sel[e], sc[e], 0.0)
        e_lo = jnp.minimum(e_lo, jnp.where(sel[e], float(e), 99.0))
        e_hi = jnp.maximum(e_hi, jnp.where(sel[e], float(e), -1.0))

    selm = sel_ref[...]
    rows = selm.shape[1]
    r = lax.broadcasted_iota(jnp.int32, (rows, rows), 0)
    c = lax.broadcasted_iota(jnp.int32, (rows, rows), 1)
    before = (r < c).astype(BF16)
    rank = jnp.dot(selm.astype(BF16), before, preferred_element_type=F32) + cnt_ref[:, 0:1]
    g_lo, g_hi, r_lo, r_hi = zero, zero, zero, zero
    for e in range(N_EXPERTS):
        gate = sc[e] / denom
        is_lo = e_lo == float(e)
        is_hi = e_hi == float(e)
        g_lo = g_lo + jnp.where(is_lo, gate, 0.0)
        g_hi = g_hi + jnp.where(is_hi, gate, 0.0)
        r_lo = r_lo + jnp.where(is_lo, rank[e:e + 1, :], 0.0)
        r_hi = r_hi + jnp.where(is_hi, rank[e:e + 1, :], 0.0)
    for i, v in enumerate((e_lo, e_hi, r_lo, r_hi, g_lo, g_hi, zero, zero)):
        meta_ref[i:i + 1, :] = v
    cnt_ref[...] = cnt_ref[...] + jnp.sum(selm, axis=1, keepdims=True)
    cnt_out_ref[...] = cnt_ref[...]


def _out_call(om, od, of, h, wo, g, b, rwt, rb, alpha):
    r_hi, r_lo = _split_bf16(rwt)
    n = h.shape[0]
    tm = ROW_TILE
    row = lambda w: pl.BlockSpec((tm, w), lambda i: (i, 0))
    full = lambda a: pl.BlockSpec(a.shape, lambda i: (0,) * a.ndim)
    return pl.pallas_call(
        functools.partial(_out_kernel, alpha), grid=(n // tm,),
        in_specs=[row(512), row(256), row(256), row(D_MODEL), full(wo), full(g), full(b),
                  full(r_hi), full(r_lo), full(rb)],
        out_specs=[row(D_MODEL), pl.BlockSpec((SUBLANES, tm), lambda i: (0, i)),
                   pl.BlockSpec((N_EXPERTS, LANES), lambda i: (0, 0))],
        out_shape=[jax.ShapeDtypeStruct((n, D_MODEL), F32), jax.ShapeDtypeStruct((SUBLANES, n), F32),
                   jax.ShapeDtypeStruct((N_EXPERTS, LANES), F32)],
        scratch_shapes=[pltpu.VMEM((N_EXPERTS, LANES), F32), pltpu.VMEM((N_EXPERTS, tm), F32)],
        compiler_params=_cparams(("arbitrary",)), name="outproj_router")(
            om, od, of, h, wo, g, b, r_hi, r_lo, rb)


def _row_copy(src_ref, src_row, dst_ref, dst_row, sem):
    return pltpu.make_async_copy(src_ref.at[pl.ds(src_row, 1)], dst_ref.at[pl.ds(dst_row, 1)], sem)


def _dispatch_kernel(fill_ref, rows_ref, h_ref, xs_ref, zero_ref, sem):
    tm = h_ref.shape[0]

    @pl.when(pl.program_id(0) == 0)
    def _():
        zero_ref[...] = jnp.zeros_like(zero_ref)
        n_fixed = fill_ref.shape[0] - 1
        tail = fill_ref[n_fixed]
        n_tail = (xs_ref.shape[0] - tail) // tm

        def chunk_copy(row):
            return pltpu.make_async_copy(zero_ref, xs_ref.at[pl.ds(pl.multiple_of(row, tm), tm)], sem)

        def chunk_row(c):
            return jnp.where(c < n_fixed, fill_ref[jnp.minimum(c, n_fixed - 1)], tail + (c - n_fixed) * tm)

        def fill(c, carry):
            row = chunk_row(c)

            @pl.when(row >= 0)
            def _():
                chunk_copy(row).start()
            return carry

        def fill_wait(c, carry):
            row = chunk_row(c)

            @pl.when(row >= 0)
            def _():
                chunk_copy(row).wait()
            return carry

        lax.fori_loop(0, n_fixed + n_tail, fill, 0)
        lax.fori_loop(0, n_fixed + n_tail, fill_wait, 0)

    def start(c, carry):
        for u in range(DMA_UNROLL):
            r = c * DMA_UNROLL + u
            for s in range(2):
                _row_copy(h_ref, r, xs_ref, rows_ref[0, 0, 2 * r + s], sem).start(priority=s)
        return carry

    def wait(c, carry):
        for _ in range(2 * DMA_UNROLL):
            _row_copy(h_ref, 0, xs_ref, 0, sem).wait()
        return carry

    lax.fori_loop(0, tm // DMA_UNROLL, start, 0)
    lax.fori_loop(0, tm // DMA_UNROLL, wait, 0)


def _dispatch_call(fill, rows, h1, total_rows):
    n, d = h1.shape
    tm = ROW_TILE
    grid_spec = pltpu.PrefetchScalarGridSpec(
        num_scalar_prefetch=1, grid=(n // tm,),
        in_specs=[pl.BlockSpec((1, 1, 2 * tm), lambda i, f: (i, 0, 0), memory_space=pltpu.SMEM),
                  pl.BlockSpec((tm, d), lambda i, f: (i, 0))],
        out_specs=pl.BlockSpec(memory_space=pl.ANY),
        scratch_shapes=[pltpu.VMEM((tm, d), F32), pltpu.SemaphoreType.DMA(())])
    return pl.pallas_call(
        _dispatch_kernel, grid_spec=grid_spec,
        out_shape=jax.ShapeDtypeStruct((total_rows, d), F32),
        compiler_params=_cparams(("arbitrary",)), name="moe_dispatch")(fill, rows, h1)


def _moe_kernel(te_ref, nu_ref, x_ref, wg_ref, wu_ref, wd_ref, y_ref):
    del te_ref
    i = pl.program_id(0)

    @pl.when(i < nu_ref[0])
    def _():
        x = x_ref[...].astype(BF16)
        gate = jnp.dot(x, wg_ref[0], preferred_element_type=F32)
        up = jnp.dot(x, wu_ref[0], preferred_element_type=F32)
        a = (gate * jax.nn.sigmoid(gate) * up).astype(BF16)
        y_ref[...] = jnp.dot(a, wd_ref[0], preferred_element_type=F32)

    @pl.when(i >= nu_ref[0])
    def _():
        y_ref[...] = jnp.zeros_like(y_ref)


def _moe_call(tile_expert, n_used, xs, wg, wu, wd):
    total_rows, d = xs.shape
    tm = MOE_TILE
    grid_spec = pltpu.PrefetchScalarGridSpec(
        num_scalar_prefetch=2, grid=(total_rows // tm,),
        in_specs=[pl.BlockSpec((tm, d), lambda i, te, nu: (jnp.minimum(i, nu[0] - 1), 0)),
                  pl.BlockSpec((1, d, D_EXPERT), lambda i, te, nu: (te[i], 0, 0)),
                  pl.BlockSpec((1, d, D_EXPERT), lambda i, te, nu: (te[i], 0, 0)),
                  pl.BlockSpec((1, D_EXPERT, d), lambda i, te, nu: (te[i], 0, 0))],
        out_specs=pl.BlockSpec((tm, d), lambda i, te, nu: (i, 0)))
    return pl.pallas_call(
        _moe_kernel, grid_spec=grid_spec,
        out_shape=jax.ShapeDtypeStruct((total_rows, d), F32),
        compiler_params=_cparams(("arbitrary",)), name="moe_experts")(
            tile_expert, n_used, xs, wg, wu, wd)


def _combine_kernel(alpha, rows_ref, meta_ref, h1_ref, ys_ref, g_ref, b_ref, o_ref, ybuf_ref, sem):
    tm = h1_ref.shape[0]

    def start(c, carry):
        for u in range(DMA_UNROLL):
            r = c * DMA_UNROLL + u
            for s in range(2):
                _row_copy(ys_ref, rows_ref[0, 0, 2 * r + s], ybuf_ref.at[s], r, sem).start(priority=s)
        return carry

    def wait(c, carry):
        for _ in range(2 * DMA_UNROLL):
            _row_copy(ys_ref, 0, ybuf_ref.at[0], 0, sem).wait()
        return carry

    lax.fori_loop(0, tm // DMA_UNROLL, start, 0)
    rr = lax.broadcasted_iota(jnp.int32, (tm, tm), 0)
    cc = lax.broadcasted_iota(jnp.int32, (tm, tm), 1)
    eye = (rr == cc).astype(F32)
    gcol = lax.dot_general(eye, meta_ref[...], (((1,), (1,)), ((), ())),
                           preferred_element_type=F32, precision=HIGHEST)
    lax.fori_loop(0, tm // DMA_UNROLL, wait, 0)
    ffn = gcol[:, 4:5] * ybuf_ref[0] + gcol[:, 5:6] * ybuf_ref[1]
    o_ref[...] = _layer_norm(alpha * h1_ref[...] + ffn, g_ref[...], b_ref[...])


def _combine_call(rows, meta, h1, ys, g, b, alpha):
    n, d = h1.shape
    tm = ROW_TILE
    vec = pl.BlockSpec((1, d), lambda i: (0, 0))
    return pl.pallas_call(
        functools.partial(_combine_kernel, alpha), grid=(n // tm,),
        in_specs=[pl.BlockSpec((1, 1, 2 * tm), lambda i: (i, 0, 0), memory_space=pltpu.SMEM),
                  pl.BlockSpec((SUBLANES, tm), lambda i: (0, i)),
                  pl.BlockSpec((tm, d), lambda i: (i, 0)),
                  pl.BlockSpec(memory_space=pl.ANY), vec, vec],
        out_specs=pl.BlockSpec((tm, d), lambda i: (i, 0)),
        out_shape=jax.ShapeDtypeStruct((n, d), F32),
        scratch_shapes=[pltpu.VMEM((2, tm, d), F32), pltpu.SemaphoreType.DMA(())],
        compiler_params=_cparams(("arbitrary",)), name="moe_combine")(rows, meta, h1, ys, g, b)


def _widen_w_in(w_in):
    z = lambda c: jnp.zeros((D_MODEL, c), F32)
    c_q, c_kv = w_in[:, 0:384], w_in[:, 384:640]
    kr = w_in[:, 640:672]
    dq, dk, dv = w_in[:, 672:928], w_in[:, 928:1184], w_in[:, 1184:1440]
    fq, fk, fv = w_in[:, 1440:1696], w_in[:, 1696:1952], w_in[:, 1952:2208]
    kr_blk = jnp.concatenate([z(64), kr, z(32)], axis=1)
    kr_swp = jnp.concatenate([z(64), -kr[:, 16:32], kr[:, 0:16], z(32)], axis=1)

    def diff_swap(w):
        w4 = w.reshape(D_MODEL, 8, DIFF_QK)
        half = DIFF_ROT // 2
        sw = jnp.concatenate([-w4[:, :, half:DIFF_ROT], w4[:, :, 0:half],
                              jnp.zeros((D_MODEL, 8, DIFF_QK - DIFF_ROT), F32)], axis=2)
        return sw.reshape(D_MODEL, 256)

    def pad_heads(w):
        w4 = w.reshape(D_MODEL, FOX_HEADS, HEAD_DIM)
        return jnp.concatenate([w4, jnp.zeros_like(w4)], axis=2).reshape(D_MODEL, FOX_HEADS * 128)

    w_ff = jnp.concatenate([w_in[:, 2208:2212], z(LANES - FOX_HEADS)], axis=1)
    w1 = jnp.concatenate([c_q, c_kv, kr_blk, kr_swp, dq, diff_swap(dq), dk, diff_swap(dk), dv,
                          pad_heads(fq), pad_heads(fk), fv, w_ff], axis=1)
    assert w1.shape[1] == _C_END
    return w1.astype(BF16)


def _widen_mla(w_uq, w_ukv):
    wq = w_uq.reshape(MLA_Q_RANK, MLA_HEADS, MLA_NOPE + MLA_ROPE)
    zq = jnp.zeros((MLA_Q_RANK, MLA_HEADS, 32), F32)
    half = MLA_ROPE // 2
    q_main = jnp.concatenate([wq, zq], axis=2)
    q_swap = jnp.concatenate([jnp.zeros((MLA_Q_RANK, MLA_HEADS, MLA_NOPE), F32),
                              -wq[:, :, MLA_NOPE + half:], wq[:, :, MLA_NOPE:MLA_NOPE + half], zq], axis=2)
    wuq = jnp.concatenate([q_main.reshape(MLA_Q_RANK, 1024), q_swap.reshape(MLA_Q_RANK, 1024)], axis=1)
    wkv = w_ukv.reshape(MLA_KV_RANK, MLA_HEADS, 128)
    k_part = jnp.concatenate([wkv[:, :, :MLA_NOPE], jnp.zeros((MLA_KV_RANK, MLA_HEADS, 64), F32)], axis=2)
    v_part = wkv[:, :, MLA_NOPE:]
    wukv = jnp.concatenate([k_part.reshape(MLA_KV_RANK, 1024), v_part.reshape(MLA_KV_RANK, 512)], axis=1)
    return wuq.astype(BF16), wukv.astype(BF16)


def _fox_lane_constants():
    ones = np.zeros((1, FOX_HEADS * 128), np.float32)
    emat = np.zeros((3 * LANES, FOX_HEADS * 128), np.float32)
    for h in range(FOX_HEADS):
        for part in range(3):
            ones[0, 128 * h + HEAD_DIM + part] = 1.0
            emat[LANES * part + h, 128 * h + HEAD_DIM + part] = 1.0
    return jnp.asarray(ones), jnp.asarray(emat, dtype=BF16)


def _rope_lane_freqs():
    inv_m = ROPE_THETA ** (-jnp.arange(0, MLA_ROPE, 2, dtype=F32) / MLA_ROPE)
    f128 = jnp.concatenate([jnp.zeros((MLA_NOPE,), F32), inv_m, inv_m, jnp.zeros((32,), F32)])
    inv_d = ROPE_THETA ** (-jnp.arange(0, DIFF_ROT, 2, dtype=F32) / DIFF_ROT)
    f32w = jnp.concatenate([inv_d, inv_d, jnp.zeros((DIFF_QK - DIFF_ROT,), F32)])
    return f128.reshape(1, 128), jnp.tile(f32w, 8).reshape(1, 256)


def kernel(x, positions, ln_in_g, ln_in_b, w_in, mla_q_norm_g, mla_kv_norm_g, mla_w_uq, mla_w_ukv,
           diff_lam_q1, diff_lam_k1, diff_lam_q2, diff_lam_k2, diff_subln_g, fox_f_bias, w_out,
           ln1_g, ln1_b, router_w, router_bias, exp_w_gate, exp_w_up, exp_w_down, ln2_g, ln2_b):
    batch, seq, d = x.shape
    depth = w_in.shape[0]
    n = batch * seq
    alpha = (2 * depth) ** 0.25
    total_rows = 2 * n + N_EXPERTS * MOE_TILE
    n_tiles = total_rows // MOE_TILE
    q_scales = ((MLA_NOPE + MLA_ROPE) ** -0.5 * LOG2E, DIFF_QK ** -0.5 * LOG2E, HEAD_DIM ** -0.5 * LOG2E)

    row = lambda v: v.reshape(1, -1).astype(F32)
    f128, f256 = _rope_lane_freqs()
    tables = _rope_call(positions.reshape(n, 1).astype(jnp.int32), f128, f256)
    ones_fox, emat = _fox_lane_constants()
    rwt = router_w.T.astype(F32)
    rb = router_bias.reshape(N_EXPERTS, 1).astype(F32)

    h = _ln_call(x.reshape(n, d), row(ln_in_g), row(ln_in_b))
    for l in range(depth):
        lam_init = 0.8 - 0.6 * math.exp(-0.3 * l)
        lam = (jnp.exp(jnp.sum(diff_lam_q1[l].astype(F32) * diff_lam_k1[l].astype(F32)))
               - jnp.exp(jnp.sum(diff_lam_q2[l].astype(F32) * diff_lam_k2[l].astype(F32))) + lam_init)
        w1 = _widen_w_in(w_in[l])
        wuq, wukv = _widen_mla(mla_w_uq[l], mla_w_ukv[l])
        b_ff = jnp.concatenate([fox_f_bias[l].astype(F32), jnp.zeros((LANES - FOX_HEADS,), F32)]).reshape(1, LANES)

        mq, mk, mvt, dq, dk, dvt, fq, fk, fvt = _prep_call(
            h, tables, w1, wuq, wukv, row(mla_q_norm_g[l]), row(mla_kv_norm_g[l]),
            emat, ones_fox, b_ff, batch, seq, q_scales)
        o_mla = _attn_call("mla", mq, mk, mvt, batch, seq)
        o_diff = _attn_call("diff", dq, dk, dvt, batch, seq, lam=lam.reshape(1).astype(F32),
                            lam_init=lam_init, subln_g=row(jnp.tile(diff_subln_g[l], 2)))
        o_fox = _attn_call("fox", fq, fk, fvt, batch, seq)

        h1, meta, cnt = _out_call(o_mla, o_diff, o_fox, h, w_out[l].astype(BF16), row(ln1_g[l]),
                                  row(ln1_b[l]), rwt, rb, alpha)

        counts = cnt[:, 0].astype(jnp.int32)
        padded = ((counts + MOE_TILE - 1) // MOE_TILE) * MOE_TILE
        ends = jnp.cumsum(padded)
        starts = ends - padded
        e_idx = meta[0:2].astype(jnp.int32)
        expert_ids = jnp.arange(N_EXPERTS, dtype=jnp.int32)[:, None, None]
        rows = jnp.sum(jnp.where(e_idx[None] == expert_ids, starts[:, None, None], 0), axis=0)
        rows = rows + meta[2:4].astype(jnp.int32)
        rows = rows.T.reshape(n // ROW_TILE, 1, 2 * ROW_TILE)
        tile_start = jnp.arange(n_tiles, dtype=jnp.int32) * MOE_TILE
        tile_expert = jnp.minimum(jnp.sum(tile_start[:, None] >= ends[None, :], axis=1),
                                  N_EXPERTS - 1).astype(jnp.int32)
        n_used = (ends[-1:] // MOE_TILE).astype(jnp.int32)

        last_tile = jnp.where(padded > 0, ends - MOE_TILE, -MOE_TILE)
        fill = jnp.concatenate([(last_tile[:, None] + jnp.arange(0, MOE_TILE, ROW_TILE)[None, :]).reshape(-1),
                                ends[-1:]]).astype(jnp.int32)
        xs = _dispatch_call(fill, rows, h1, total_rows)
        ys = _moe_call(tile_expert, n_used, xs, exp_w_gate[l].astype(BF16), exp_w_up[l].astype(BF16),
                       exp_w_down[l].astype(BF16))
        h = _combine_call(rows, meta, h1, ys, row(ln2_g[l]), row(ln2_b[l]), alpha)
    return h.reshape(batch, seq, d)
```

```python
import functools
import math

import numpy as np
import jax
import jax.numpy as jnp
from jax import lax
from jax.experimental import pallas as pl
from jax.experimental.pallas import tpu as pltpu

F32 = jnp.float32
BF16 = jnp.bfloat16

D_MODEL = 1024
HEAD_DIM = 64
MLA_HEADS = 8
DIFF_HEADS = 4
FOX_HEADS = 4
MLA_Q_RANK = 384
MLA_KV_RANK = 256
MLA_NOPE = 64
MLA_ROPE = 32
DIFF_QK = 32
DIFF_ROT = 8
N_EXPERTS = 16
N_GROUPS = 4
GROUP_SIZE = 4
D_EXPERT = 512
ROPE_THETA = 500000.0
CHUNK = 64
LN_EPS = 1e-5
RMS_EPS = 1e-6
NEG_INF = -1e30
LOG2E = 1.4426950408889634

LANES = 128
SUBLANES = 8
VMEM_LIMIT_BYTES = 56 * 1024 * 1024

ROW_TILE = 256
ATTN_TILE = 256
OUT_TILE = 512
MOE_TILE = 512
KV_TILES_PER_STEP = 4
SUM_ROWS = 16
DMA_UNROLL = 16

_C_CQ, _C_CKV, _C_KR, _C_KRS = 0, 384, 640, 768
_C_DQ, _C_DQS, _C_DK, _C_DKS, _C_DV = 896, 1152, 1408, 1664, 1920
_C_FQ, _C_FK, _C_FV, _C_FF, _C_END = 2176, 2688, 3200, 3456, 3584


def _cparams(sem):
    return pltpu.CompilerParams(dimension_semantics=sem, vmem_limit_bytes=VMEM_LIMIT_BYTES)


def _split_bf16(x):
    hi = x.astype(BF16)
    return hi, (x - hi.astype(F32)).astype(BF16)


def _layer_norm(x, g, b):
    mu = jnp.mean(x, axis=-1, keepdims=True)
    xc = x - mu
    var = jnp.mean(xc * xc, axis=-1, keepdims=True)
    return xc * lax.rsqrt(var + LN_EPS) * g + b


def _rope_kernel(pos_ref, freq_ref, e128_ref, e256_ref, c128_ref, s128_ref, c256_ref, s256_ref):
    a = pos_ref[...].astype(F32) * freq_ref[...]
    cos3 = _split3_bf16(jnp.cos(a))
    sin3 = _split3_bf16(jnp.sin(a))
    c128_ref[...] = jnp.dot(cos3, e128_ref[...], preferred_element_type=F32)
    s128_ref[...] = jnp.dot(sin3, e128_ref[...], preferred_element_type=F32)
    c256_ref[...] = jnp.dot(cos3, e256_ref[...], preferred_element_type=F32)
    s256_ref[...] = jnp.dot(sin3, e256_ref[...], preferred_element_type=F32)


def _rope_call(pos, freq, e128, e256):
    n = pos.shape[0]
    spec = lambda w: pl.BlockSpec((ROW_TILE, w), lambda i: (i, 0))
    full = lambda a: pl.BlockSpec(a.shape, lambda i: (0,) * a.ndim)
    return pl.pallas_call(
        _rope_kernel, grid=(n // ROW_TILE,),
        in_specs=[spec(1), full(freq), full(e128), full(e256)],
        out_specs=[spec(128), spec(128), spec(256), spec(256)],
        out_shape=[jax.ShapeDtypeStruct((n, 128), F32), jax.ShapeDtypeStruct((n, 128), F32),
                   jax.ShapeDtypeStruct((n, 256), F32), jax.ShapeDtypeStruct((n, 256), F32)],
        compiler_params=_cparams(("parallel",)), name="rope_tables")(pos, freq, e128, e256)


def _split3_bf16(x):
    hi = x.astype(BF16)
    r1 = x - hi.astype(F32)
    mid = r1.astype(BF16)
    lo = (r1 - mid.astype(F32)).astype(BF16)
    return jnp.concatenate([hi, mid, lo], axis=1)


def _prep_kernel(q_scales, tiles_per_seq, norm_input, *refs):
    if norm_input:
        x_ref, lng_ref, lnb_ref, *refs = refs
    else:
        x_ref, *refs = refs
    (c128_ref, s128_ref, c256_ref, s256_ref, w1_ref, wuq_ref, wukv_ref, gq_ref, gkv_ref, e_ref, ones_ref,
     bff_ref, *refs) = refs
    if norm_input:
        hout_ref, *refs = refs
    mq_ref, mk_ref, mvt_ref, dq_ref, dk_ref, dvt_ref, fq_ref, fk_ref, fvt_ref, carry_ref = refs
    qs_mla, qs_diff, qs_fox = q_scales
    if norm_input:
        h = _layer_norm(x_ref[...], lng_ref[...], lnb_ref[...])
        hout_ref[...] = h
    else:
        h = x_ref[...]
    proj = jnp.dot(h.astype(BF16), w1_ref[...], preferred_element_type=F32)
    cos_m = c128_ref[...]
    sin_m = s128_ref[...]

    def rms(x, g):
        return (x * lax.rsqrt(jnp.mean(x * x, axis=-1, keepdims=True) + RMS_EPS) * g).astype(BF16)

    qe = jnp.dot(rms(proj[:, _C_CQ:_C_CKV], gq_ref[...]), wuq_ref[...], preferred_element_type=F32)
    for h in range(MLA_HEADS):
        q = qe[:, 128 * h:128 * h + 128] * cos_m + qe[:, 1024 + 128 * h:1152 + 128 * h] * sin_m
        mq_ref[0, h] = (q * qs_mla).astype(BF16)
    kve = jnp.dot(rms(proj[:, _C_CKV:_C_KR], gkv_ref[...]), wukv_ref[...], preferred_element_type=F32)
    kr = proj[:, _C_KR:_C_KRS] * cos_m + proj[:, _C_KRS:_C_DQ] * sin_m
    for h in range(MLA_HEADS):
        mk_ref[0, h] = (kve[:, 128 * h:128 * h + 128] + kr).astype(BF16)
    for p in range(MLA_HEADS // 2):
        mvt_ref[0, p, 0] = kve[:, 1024 + 128 * p:1152 + 128 * p].T.astype(BF16)

    cos_d = c256_ref[...]
    sin_d = s256_ref[...]
    dq = (proj[:, _C_DQ:_C_DQS] * cos_d + proj[:, _C_DQS:_C_DK] * sin_d) * qs_diff
    dk = proj[:, _C_DK:_C_DKS] * cos_d + proj[:, _C_DKS:_C_DV] * sin_d
    for p in range(DIFF_HEADS // 2):
        dq_ref[0, p] = dq[:, 128 * p:128 * p + 128].astype(BF16)
        dk_ref[0, p] = dk[:, 128 * p:128 * p + 128].astype(BF16)
        dvt_ref[0, p, 0] = proj[:, _C_DV + 128 * p:_C_DV + 128 * p + 128].T.astype(BF16)

    fq = proj[:, _C_FQ:_C_FK] * qs_fox + ones_ref[...]

    @pl.when(pl.program_id(0) % tiles_per_seq == 0)
    def _():
        carry_ref[...] = jnp.zeros_like(carry_ref)

    logit = proj[:, _C_FF:_C_END] + bff_ref[...]
    log_f = jnp.minimum(logit, 0.0) - jnp.log1p(jnp.exp(-jnp.abs(logit)))
    rows = log_f.shape[0]
    tri = (lax.broadcasted_iota(jnp.int32, (rows, rows), 0)
           >= lax.broadcasted_iota(jnp.int32, (rows, rows), 1)).astype(BF16)
    c3 = jnp.dot(tri, _split3_bf16(log_f), preferred_element_type=F32)
    cum = c3[:, 0:LANES] + c3[:, LANES:2 * LANES] + c3[:, 2 * LANES:3 * LANES] + carry_ref[...]
    carry_ref[...] = cum[rows - 1:rows, :]
    fkb = jnp.dot(_split3_bf16(cum * (-LOG2E)), e_ref[...], preferred_element_type=F32)
    fk = proj[:, _C_FK:_C_FV] + fkb
    for h in range(FOX_HEADS):
        fq_ref[0, h] = fq[:, 128 * h:128 * h + 128].astype(BF16)
        fk_ref[0, h] = fk[:, 128 * h:128 * h + 128].astype(BF16)
    for p in range(FOX_HEADS // 2):
        fvt_ref[0, p, 0] = proj[:, _C_FV + 128 * p:_C_FV + 128 * p + 128].T.astype(BF16)


def _prep_call(h, tables, w1, wuq, wukv, gq, gkv, emat, ones, b_ff, batch, seq, q_scales, ln_in=None):
    tm = ROW_TILE
    nt = seq // tm
    c128, s128, c256, s256 = tables
    row = lambda w: pl.BlockSpec((tm, w), lambda i: (i, 0))
    full = lambda a: pl.BlockSpec(a.shape, lambda i: (0,) * a.ndim)
    head = lambda nh: pl.BlockSpec((1, nh, tm, 128), lambda i: (i // nt, 0, i % nt, 0))
    vt = lambda npair: pl.BlockSpec((1, npair, 1, 128, tm), lambda i: (i // nt, 0, i % nt, 0, 0))
    hshape = lambda nh: jax.ShapeDtypeStruct((batch, nh, seq, 128), BF16)
    vshape = lambda npair: jax.ShapeDtypeStruct((batch, npair, nt, 128, tm), BF16)
    norm_input = ln_in is not None
    ln_args = list(ln_in) if norm_input else []
    weights = [w1, wuq, wukv, gq, gkv, emat, ones, b_ff]
    return pl.pallas_call(
        functools.partial(_prep_kernel, q_scales, nt, norm_input), grid=(batch * nt,),
        in_specs=([row(D_MODEL)] + [full(a) for a in ln_args] + [row(128), row(128), row(256), row(256)]
                  + [full(a) for a in weights]),
        out_specs=([row(D_MODEL)] if norm_input else []) + [
            head(8), head(8), vt(4), head(2), head(2), vt(2), head(4), head(4), vt(2)],
        out_shape=([jax.ShapeDtypeStruct(h.shape, F32)] if norm_input else []) + [
            hshape(8), hshape(8), vshape(4), hshape(2), hshape(2), vshape(2), hshape(4), hshape(4), vshape(2)],
        scratch_shapes=[pltpu.VMEM((1, LANES), F32)],
        compiler_params=_cparams(("arbitrary",)), name="prep")(
            h, *ln_args, c128, s128, c256, s256, *weights)


def _attn_kernel(kind, lam_init, *refs):
    if kind == "diff":
        lam_ref, q_ref, k_ref, vt_ref, g_ref, o_ref, m_ref, acc_ref, s_ref, bm_ref = refs
    else:
        q_ref, k_ref, vt_ref, o_ref, m_ref, acc_ref, s_ref, bm_ref = refs
    t = ATTN_TILE
    qi = pl.program_id(1)
    npair = vt_ref.shape[1]

    qs, k_of_map, v_of_map = [], [], []
    if kind == "diff":
        for p in range(npair):
            q2 = q_ref[0, p]
            lane = lax.broadcasted_iota(jnp.int32, q2.shape, 1)
            for sub in range(4):
                keep = (lane >= DIFF_QK * sub) & (lane < DIFF_QK * (sub + 1))
                qs.append(jnp.where(keep, q2, jnp.zeros_like(q2)))
                k_of_map.append(p)
                v_of_map.append((p, sub // 2))
    else:
        for h in range(2 * npair):
            qs.append(q_ref[0, h])
            k_of_map.append(h)
            v_of_map.append((h // 2, h % 2))
    nmap = len(qs)

    m_ref[...] = jnp.full(m_ref.shape, NEG_INF, F32)
    acc_ref[...] = jnp.zeros(acc_ref.shape, F32)

    kk = lax.broadcasted_iota(jnp.int32, (t, t), 0)
    qq = lax.broadcasted_iota(jnp.int32, (t, t), 1)
    if kind == "fox":
        diag_mask = kk <= qq
    else:
        diag_mask = (kk // CHUNK) <= (qq // CHUNK)
    ones_rows = jnp.ones((SUM_ROWS, t), BF16)

    def score_block(j, slot):
        start = pl.multiple_of(j * t, t)
        for m in range(nmap):
            kblk = k_ref[0, k_of_map[m], pl.ds(start, t), :]
            s = lax.dot_general(kblk, qs[m], (((1,), (1,)), ((), ())), preferred_element_type=F32)
            s_ref[slot, m] = s
            bm_ref[slot, m] = jnp.max(s, axis=0, keepdims=True)

    def softmax_block(j, slot, masked):
        probs, alphas = [], []
        for m in range(nmap):
            s = s_ref[slot, m]
            m_old = m_ref[m]
            if masked:
                s = jnp.where(diag_mask, s, NEG_INF)
                m_new = jnp.maximum(m_old, jnp.max(s, axis=0, keepdims=True))
            else:
                m_new = jnp.maximum(m_old, bm_ref[slot, m])
            alphas.append(jnp.exp2(m_old - m_new))
            probs.append(jnp.exp2(s - m_new).astype(BF16))
            m_ref[m] = m_new
        for m in range(nmap):
            pair, hd = v_of_map[m]
            vblk = jnp.concatenate([vt_ref[0, pair, j, HEAD_DIM * hd:HEAD_DIM * (hd + 1), :], ones_rows],
                                   axis=0)
            pv = jnp.dot(vblk, probs[m], preferred_element_type=F32)
            acc_ref[m] = alphas[m] * acc_ref[m] + pv

    score_block(0, 0)
    unroll = KV_TILES_PER_STEP

    def tiles(jj, carry):
        j = unroll * jj
        for u in range(unroll):
            score_block(j + u + 1, (u + 1) % 2)
            softmax_block(j + u, u % 2, False)
        return carry

    lax.fori_loop(0, qi // unroll, tiles, 0)
    base = qi - qi % unroll
    for rem in range(unroll):
        @pl.when(qi % unroll == rem)
        def _():
            for u in range(rem):
                score_block(base + u + 1, (u + 1) % 2)
                softmax_block(base + u, u % 2, False)
            softmax_block(qi, rem % 2, True)

    def normalized(m):
        return acc_ref[m, 0:HEAD_DIM, :] / acc_ref[m, HEAD_DIM:HEAD_DIM + 1, :]

    for p in range(npair):
        if kind == "diff":
            lam = lam_ref[0]
            outs = []
            for hd in range(2):
                o = normalized(4 * p + 2 * hd) - lam * normalized(4 * p + 2 * hd + 1)
                outs.append(o * lax.rsqrt(jnp.mean(o * o, axis=0, keepdims=True) + RMS_EPS))
            ot = jnp.concatenate(outs, axis=0).T
            ot = ot * g_ref[...] * (1.0 - lam_init)
        else:
            ot = jnp.concatenate([normalized(2 * p), normalized(2 * p + 1)], axis=0).T
        o_ref[:, 128 * p:128 * (p + 1)] = ot.astype(BF16)


def _attn_call(kind, q, k, vt, batch, seq, lam=None, lam_init=0.0, subln_g=None):
    t = ATTN_TILE
    nq = seq // t
    npair = vt.shape[1]
    nslab = q.shape[1]
    nmap = 4 * npair if kind == "diff" else 2 * npair
    resident = pl.Buffered(1)
    in_specs = [pl.BlockSpec((1, nslab, t, 128), lambda b, i: (b, 0, i, 0)),
                pl.BlockSpec((1, nslab, seq, 128), lambda b, i: (b, 0, 0, 0), pipeline_mode=resident),
                pl.BlockSpec((1, npair, nq, 128, t), lambda b, i: (b, 0, 0, 0, 0), pipeline_mode=resident)]
    args = [q, k, vt]
    if kind == "diff":
        in_specs = [pl.BlockSpec(memory_space=pltpu.SMEM)] + in_specs + [
            pl.BlockSpec((1, 128), lambda b, i: (0, 0))]
        args = [lam] + args + [subln_g]
    return pl.pallas_call(
        functools.partial(_attn_kernel, kind, lam_init), grid=(batch, nq),
        in_specs=in_specs,
        out_specs=pl.BlockSpec((t, 128 * npair), lambda b, i: (b * nq + i, 0)),
        out_shape=jax.ShapeDtypeStruct((batch * seq, 128 * npair), BF16),
        scratch_shapes=[pltpu.VMEM((nmap, 1, t), F32),
                        pltpu.VMEM((nmap, HEAD_DIM + SUM_ROWS, t), F32),
                        pltpu.VMEM((2, nmap, t, t), F32), pltpu.VMEM((2, nmap, 1, t), F32)],
        compiler_params=_cparams(("parallel", "arbitrary")),
        name="attn_" + kind)(*args)


def _out_kernel(alpha, om_ref, od_ref, of_ref, h_ref, wo_ref, g_ref, b_ref, rhi_ref, rlo_ref, rb_ref,
                h1_ref, meta_ref, cnt_out_ref, cnt_ref, sel_ref):
    @pl.when(pl.program_id(0) == 0)
    def _():
        cnt_ref[...] = jnp.zeros_like(cnt_ref)

    mix = (jnp.dot(om_ref[...], wo_ref[0:512, :], preferred_element_type=F32)
           + jnp.dot(od_ref[...], wo_ref[512:768, :], preferred_element_type=F32)
           + jnp.dot(of_ref[...], wo_ref[768:1024, :], preferred_element_type=F32))
    h1 = _layer_norm(alpha * h_ref[...] + mix, g_ref[...], b_ref[...])
    h1_ref[...] = h1

    h_hi, h_lo = _split_bf16(h1)
    nt = lambda x, y: lax.dot_general(x, y, (((1,), (1,)), ((), ())), preferred_element_type=F32)
    logits = nt(rhi_ref[...], h_hi) + nt(rhi_ref[...], h_lo) + nt(rlo_ref[...], h_hi)
    score = jax.nn.sigmoid(logits)
    biased = score + rb_ref[...]
    b = [biased[e:e + 1, :] for e in range(N_EXPERTS)]
    sc = [score[e:e + 1, :] for e in range(N_EXPERTS)]

    gscore = []
    for g in range(N_GROUPS):
        x = b[GROUP_SIZE * g:GROUP_SIZE * (g + 1)]
        pair_sums = [x[i] + x[j] for i in range(GROUP_SIZE) for j in range(i + 1, GROUP_SIZE)]
        gscore.append(functools.reduce(jnp.maximum, pair_sums))
    sel = []
    for g in range(N_GROUPS):
        first_max = None
        for g2 in range(N_GROUPS):
            if g2 == g:
                continue
            c = (gscore[g] > gscore[g2]) if g2 < g else (gscore[g] >= gscore[g2])
            first_max = c if first_max is None else (first_max & c)
        x = b[GROUP_SIZE * g:GROUP_SIZE * (g + 1)]
        for j in range(GROUP_SIZE):
            ahead = jnp.zeros_like(x[j])
            for i in range(GROUP_SIZE):
                if i < j:
                    ahead = ahead + (x[i] >= x[j]).astype(F32)
                elif i > j:
                    ahead = ahead + (x[i] > x[j]).astype(F32)
            sel.append(first_max & (ahead < 2.0))

    zero = jnp.zeros_like(sc[0])
    denom = zero
    e_lo = jnp.full_like(zero, 99.0)
    e_hi = jnp.full_like(zero, -1.0)
    for e in range(N_EXPERTS):
        sel_ref[e:e + 1, :] = sel[e].astype(F32)
        denom = denom + jnp.where(sel[e], sc[e], 0.0)
        e_lo = jnp.minimum(e_lo, jnp.where(sel[e], float(e), 99.0))
        e_hi = jnp.maximum(e_hi, jnp.where(sel[e], float(e), -1.0))

    selm = sel_ref[...]
    rows = selm.shape[1]
    r = lax.broadcasted_iota(jnp.int32, (rows, rows), 0)
    c = lax.broadcasted_iota(jnp.int32, (rows, rows), 1)
    before = (r < c).astype(BF16)
    rank = jnp.dot(selm.astype(BF16), before, preferred_element_type=F32) + cnt_ref[:, 0:1]
    g_lo, g_hi, r_lo, r_hi = zero, zero, zero, zero
    for e in range(N_EXPERTS):
        gate = sc[e] / denom
        is_lo = e_lo == float(e)
        is_hi = e_hi == float(e)
        g_lo = g_lo + jnp.where(is_lo, gate, 0.0)
        g_hi = g_hi + jnp.where(is_hi, gate, 0.0)
        r_lo = r_lo + jnp.where(is_lo, rank[e:e + 1, :], 0.0)
        r_hi = r_hi + jnp.where(is_hi, rank[e:e + 1, :], 0.0)
    for i, v in enumerate((e_lo, e_hi, r_lo, r_hi, g_lo, g_hi, zero, zero)):
        meta_ref[i:i + 1, :] = v
    cnt_ref[...] = cnt_ref[...] + jnp.sum(selm, axis=1, keepdims=True)
    cnt_out_ref[...] = cnt_ref[...]


def _out_call(om, od, of, h, wo, g, b, rwt, rb, alpha):
    r_hi, r_lo = _split_bf16(rwt)
    n = h.shape[0]
    tm = OUT_TILE
    row = lambda w: pl.BlockSpec((tm, w), lambda i: (i, 0))
    full = lambda a: pl.BlockSpec(a.shape, lambda i: (0,) * a.ndim)
    return pl.pallas_call(
        functools.partial(_out_kernel, alpha), grid=(n // tm,),
        in_specs=[row(512), row(256), row(256), row(D_MODEL), full(wo), full(g), full(b),
                  full(r_hi), full(r_lo), full(rb)],
        out_specs=[row(D_MODEL), pl.BlockSpec((SUBLANES, tm), lambda i: (0, i)),
                   pl.BlockSpec((N_EXPERTS, LANES), lambda i: (0, 0))],
        out_shape=[jax.ShapeDtypeStruct((n, D_MODEL), F32), jax.ShapeDtypeStruct((SUBLANES, n), F32),
                   jax.ShapeDtypeStruct((N_EXPERTS, LANES), F32)],
        scratch_shapes=[pltpu.VMEM((N_EXPERTS, LANES), F32), pltpu.VMEM((N_EXPERTS, tm), F32)],
        compiler_params=_cparams(("arbitrary",)), name="outproj_router")(
            om, od, of, h, wo, g, b, r_hi, r_lo, rb)


def _row_copy(src_ref, src_row, dst_ref, dst_row, sem):
    return pltpu.make_async_copy(src_ref.at[pl.ds(src_row, 1)], dst_ref.at[pl.ds(dst_row, 1)], sem)


def _dispatch_kernel(fill_ref, rows_ref, h_ref, xs_ref, zero_ref, sem):
    tm = h_ref.shape[0]

    @pl.when(pl.program_id(0) == 0)
    def _():
        zero_ref[...] = jnp.zeros_like(zero_ref)
        n_fixed = fill_ref.shape[0] - 1
        tail = fill_ref[n_fixed]
        n_tail = (xs_ref.shape[0] - tail) // tm

        def chunk_copy(row):
            return pltpu.make_async_copy(zero_ref, xs_ref.at[pl.ds(pl.multiple_of(row, tm), tm)], sem)

        def chunk_row(c):
            return jnp.where(c < n_fixed, fill_ref[jnp.minimum(c, n_fixed - 1)], tail + (c - n_fixed) * tm)

        def fill(c, carry):
            row = chunk_row(c)

            @pl.when(row >= 0)
            def _():
                chunk_copy(row).start()
            return carry

        def fill_wait(c, carry):
            row = chunk_row(c)

            @pl.when(row >= 0)
            def _():
                chunk_copy(row).wait()
            return carry

        lax.fori_loop(0, n_fixed + n_tail, fill, 0)
        lax.fori_loop(0, n_fixed + n_tail, fill_wait, 0)

    def start(c, carry):
        for u in range(DMA_UNROLL):
            r = c * DMA_UNROLL + u
            for s in range(2):
                _row_copy(h_ref, r, xs_ref, rows_ref[0, 0, 2 * r + s], sem).start(priority=s)
        return carry

    def wait(c, carry):
        for _ in range(2 * DMA_UNROLL):
            _row_copy(h_ref, 0, xs_ref, 0, sem).wait()
        return carry

    lax.fori_loop(0, tm // DMA_UNROLL, start, 0)
    lax.fori_loop(0, tm // DMA_UNROLL, wait, 0)


def _dispatch_call(fill, rows, h1, total_rows):
    n, d = h1.shape
    tm = ROW_TILE
    grid_spec = pltpu.PrefetchScalarGridSpec(
        num_scalar_prefetch=1, grid=(n // tm,),
        in_specs=[pl.BlockSpec((1, 1, 2 * tm), lambda i, f: (i, 0, 0), memory_space=pltpu.SMEM),
                  pl.BlockSpec((tm, d), lambda i, f: (i, 0))],
        out_specs=pl.BlockSpec(memory_space=pl.ANY),
        scratch_shapes=[pltpu.VMEM((tm, d), F32), pltpu.SemaphoreType.DMA(())])
    return pl.pallas_call(
        _dispatch_kernel, grid_spec=grid_spec,
        out_shape=jax.ShapeDtypeStruct((total_rows, d), F32),
        compiler_params=_cparams(("arbitrary",)), name="moe_dispatch")(fill, rows, h1)


def _moe_kernel(te_ref, nu_ref, x_ref, wg_ref, wu_ref, wd_ref, y_ref):
    del te_ref
    i = pl.program_id(0)

    @pl.when(i < nu_ref[0])
    def _():
        x = x_ref[...].astype(BF16)
        gate = jnp.dot(x, wg_ref[0], preferred_element_type=F32)
        up = jnp.dot(x, wu_ref[0], preferred_element_type=F32)
        a = (gate * jax.nn.sigmoid(gate) * up).astype(BF16)
        y_ref[...] = jnp.dot(a, wd_ref[0], preferred_element_type=F32)

    @pl.when(i >= nu_ref[0])
    def _():
        y_ref[...] = jnp.zeros_like(y_ref)


def _moe_call(tile_expert, n_used, xs, wg, wu, wd):
    total_rows, d = xs.shape
    tm = MOE_TILE
    grid_spec = pltpu.PrefetchScalarGridSpec(
        num_scalar_prefetch=2, grid=(total_rows // tm,),
        in_specs=[pl.BlockSpec((tm, d), lambda i, te, nu: (jnp.minimum(i, nu[0] - 1), 0)),
                  pl.BlockSpec((1, d, D_EXPERT), lambda i, te, nu: (te[i], 0, 0)),
                  pl.BlockSpec((1, d, D_EXPERT), lambda i, te, nu: (te[i], 0, 0)),
                  pl.BlockSpec((1, D_EXPERT, d), lambda i, te, nu: (te[i], 0, 0))],
        out_specs=pl.BlockSpec((tm, d), lambda i, te, nu: (i, 0)))
    return pl.pallas_call(
        _moe_kernel, grid_spec=grid_spec,
        out_shape=jax.ShapeDtypeStruct((total_rows, d), F32),
        compiler_params=_cparams(("arbitrary",)), name="moe_experts")(
            tile_expert, n_used, xs, wg, wu, wd)


def _combine_kernel(alpha, rows_ref, meta_ref, h1_ref, ys_ref, g_ref, b_ref, o_ref, ybuf_ref, sem):
    tm = h1_ref.shape[0]

    def start(c, carry):
        for u in range(DMA_UNROLL):
            r = c * DMA_UNROLL + u
            for s in range(2):
                _row_copy(ys_ref, rows_ref[0, 0, 2 * r + s], ybuf_ref.at[s], r, sem).start(priority=s)
        return carry

    def wait(c, carry):
        for _ in range(2 * DMA_UNROLL):
            _row_copy(ys_ref, 0, ybuf_ref.at[0], 0, sem).wait()
        return carry

    lax.fori_loop(0, tm // DMA_UNROLL, start, 0)
    meta_rows = jnp.concatenate([meta_ref[...], jnp.zeros((LANES - SUBLANES, tm), F32)], axis=0)
    gcol = meta_rows.T
    lax.fori_loop(0, tm // DMA_UNROLL, wait, 0)
    ffn = gcol[:, 4:5] * ybuf_ref[0] + gcol[:, 5:6] * ybuf_ref[1]
    o_ref[...] = _layer_norm(alpha * h1_ref[...] + ffn, g_ref[...], b_ref[...])


def _combine_call(rows, meta, h1, ys, g, b, alpha):
    n, d = h1.shape
    tm = ROW_TILE
    vec = pl.BlockSpec((1, d), lambda i: (0, 0))
    return pl.pallas_call(
        functools.partial(_combine_kernel, alpha), grid=(n // tm,),
        in_specs=[pl.BlockSpec((1, 1, 2 * tm), lambda i: (i, 0, 0), memory_space=pltpu.SMEM),
                  pl.BlockSpec((SUBLANES, tm), lambda i: (0, i)),
                  pl.BlockSpec((tm, d), lambda i: (i, 0)),
                  pl.BlockSpec(memory_space=pl.ANY), vec, vec],
        out_specs=pl.BlockSpec((tm, d), lambda i: (i, 0)),
        out_shape=jax.ShapeDtypeStruct((n, d), F32),
        scratch_shapes=[pltpu.VMEM((2, tm, d), F32), pltpu.SemaphoreType.DMA(())],
        compiler_params=_cparams(("arbitrary",)), name="moe_combine")(rows, meta, h1, ys, g, b)


def _widen_w_in(w_in):
    z = lambda c: jnp.zeros((D_MODEL, c), F32)
    c_q, c_kv = w_in[:, 0:384], w_in[:, 384:640]
    kr = w_in[:, 640:672]
    dq, dk, dv = w_in[:, 672:928], w_in[:, 928:1184], w_in[:, 1184:1440]
    fq, fk, fv = w_in[:, 1440:1696], w_in[:, 1696:1952], w_in[:, 1952:2208]
    kr_blk = jnp.concatenate([z(64), kr, z(32)], axis=1)
    kr_swp = jnp.concatenate([z(64), -kr[:, 16:32], kr[:, 0:16], z(32)], axis=1)

    def diff_swap(w):
        w4 = w.reshape(D_MODEL, 8, DIFF_QK)
        half = DIFF_ROT // 2
        sw = jnp.concatenate([-w4[:, :, half:DIFF_ROT], w4[:, :, 0:half],
                              jnp.zeros((D_MODEL, 8, DIFF_QK - DIFF_ROT), F32)], axis=2)
        return sw.reshape(D_MODEL, 256)

    def pad_heads(w):
        w4 = w.reshape(D_MODEL, FOX_HEADS, HEAD_DIM)
        return jnp.concatenate([w4, jnp.zeros_like(w4)], axis=2).reshape(D_MODEL, FOX_HEADS * 128)

    w_ff = jnp.concatenate([w_in[:, 2208:2212], z(LANES - FOX_HEADS)], axis=1)
    w1 = jnp.concatenate([c_q, c_kv, kr_blk, kr_swp, dq, diff_swap(dq), dk, diff_swap(dk), dv,
                          pad_heads(fq), pad_heads(fk), fv, w_ff], axis=1)
    assert w1.shape[1] == _C_END
    return w1.astype(BF16)


def _widen_mla(w_uq, w_ukv):
    wq = w_uq.reshape(MLA_Q_RANK, MLA_HEADS, MLA_NOPE + MLA_ROPE)
    zq = jnp.zeros((MLA_Q_RANK, MLA_HEADS, 32), F32)
    half = MLA_ROPE // 2
    q_main = jnp.concatenate([wq, zq], axis=2)
    q_swap = jnp.concatenate([jnp.zeros((MLA_Q_RANK, MLA_HEADS, MLA_NOPE), F32),
                              -wq[:, :, MLA_NOPE + half:], wq[:, :, MLA_NOPE:MLA_NOPE + half], zq], axis=2)
    wuq = jnp.concatenate([q_main.reshape(MLA_Q_RANK, 1024), q_swap.reshape(MLA_Q_RANK, 1024)], axis=1)
    wkv = w_ukv.reshape(MLA_KV_RANK, MLA_HEADS, 128)
    k_part = jnp.concatenate([wkv[:, :, :MLA_NOPE], jnp.zeros((MLA_KV_RANK, MLA_HEADS, 64), F32)], axis=2)
    v_part = wkv[:, :, MLA_NOPE:]
    wukv = jnp.concatenate([k_part.reshape(MLA_KV_RANK, 1024), v_part.reshape(MLA_KV_RANK, 512)], axis=1)
    return wuq.astype(BF16), wukv.astype(BF16)


def _fox_lane_constants():
    ones = np.zeros((1, FOX_HEADS * 128), np.float32)
    emat = np.zeros((3 * LANES, FOX_HEADS * 128), np.float32)
    for h in range(FOX_HEADS):
        for part in range(3):
            ones[0, 128 * h + HEAD_DIM + part] = 1.0
            emat[LANES * part + h, 128 * h + HEAD_DIM + part] = 1.0
    return jnp.asarray(ones), jnp.asarray(emat, dtype=BF16)


def _rope_lane_constants():
    n_m, n_d = MLA_ROPE // 2, DIFF_ROT // 2
    inv_m = ROPE_THETA ** (-jnp.arange(0, MLA_ROPE, 2, dtype=F32) / MLA_ROPE)
    inv_d = ROPE_THETA ** (-jnp.arange(0, DIFF_ROT, 2, dtype=F32) / DIFF_ROT)
    freq = jnp.concatenate([inv_m, inv_d, jnp.zeros((LANES - n_m - n_d,), F32)]).reshape(1, LANES)
    src128 = np.full((128,), LANES - 1)
    src128[MLA_NOPE:MLA_NOPE + n_m] = np.arange(n_m)
    src128[MLA_NOPE + n_m:MLA_NOPE + 2 * n_m] = np.arange(n_m)
    src32 = np.full((DIFF_QK,), LANES - 1)
    src32[0:n_d] = n_m + np.arange(n_d)
    src32[n_d:2 * n_d] = n_m + np.arange(n_d)
    src256 = np.tile(src32, 8)

    def copy_matrix(src):
        e = np.zeros((LANES, src.shape[0]), np.float32)
        e[src, np.arange(src.shape[0])] = 1.0
        return jnp.asarray(np.concatenate([e, e, e], axis=0), dtype=BF16)

    return freq, copy_matrix(src128), copy_matrix(src256)


def kernel(x, positions, ln_in_g, ln_in_b, w_in, mla_q_norm_g, mla_kv_norm_g, mla_w_uq, mla_w_ukv,
           diff_lam_q1, diff_lam_k1, diff_lam_q2, diff_lam_k2, diff_subln_g, fox_f_bias, w_out,
           ln1_g, ln1_b, router_w, router_bias, exp_w_gate, exp_w_up, exp_w_down, ln2_g, ln2_b):
    batch, seq, d = x.shape
    depth = w_in.shape[0]
    n = batch * seq
    alpha = (2 * depth) ** 0.25
    total_rows = 2 * n + N_EXPERTS * MOE_TILE
    n_tiles = total_rows // MOE_TILE
    q_scales = ((MLA_NOPE + MLA_ROPE) ** -0.5 * LOG2E, DIFF_QK ** -0.5 * LOG2E, HEAD_DIM ** -0.5 * LOG2E)

    row = lambda v: v.reshape(1, -1).astype(F32)
    tables = _rope_call(positions.reshape(n, 1).astype(jnp.int32), *_rope_lane_constants())
    ones_fox, emat = _fox_lane_constants()
    rwt = router_w.T.astype(F32)
    rb = router_bias.reshape(N_EXPERTS, 1).astype(F32)

    h = x.reshape(n, d).astype(F32)
    for l in range(depth):
        lam_init = 0.8 - 0.6 * math.exp(-0.3 * l)
        lam = (jnp.exp(jnp.sum(diff_lam_q1[l].astype(F32) * diff_lam_k1[l].astype(F32)))
               - jnp.exp(jnp.sum(diff_lam_q2[l].astype(F32) * diff_lam_k2[l].astype(F32))) + lam_init)
        w1 = _widen_w_in(w_in[l])
        wuq, wukv = _widen_mla(mla_w_uq[l], mla_w_ukv[l])
        b_ff = jnp.concatenate([fox_f_bias[l].astype(F32), jnp.zeros((LANES - FOX_HEADS,), F32)]).reshape(1, LANES)

        prep_out = _prep_call(h, tables, w1, wuq, wukv, row(mla_q_norm_g[l]), row(mla_kv_norm_g[l]),
                              emat, ones_fox, b_ff, batch, seq, q_scales,
                              ln_in=(row(ln_in_g), row(ln_in_b)) if l == 0 else None)
        if l == 0:
            h, *prep_out = prep_out
        mq, mk, mvt, dq, dk, dvt, fq, fk, fvt = prep_out
        o_mla = _attn_call("mla", mq, mk, mvt, batch, seq)
        o_diff = _attn_call("diff", dq, dk, dvt, batch, seq, lam=lam.reshape(1).astype(F32),
                            lam_init=lam_init, subln_g=row(jnp.tile(diff_subln_g[l], 2)))
        o_fox = _attn_call("fox", fq, fk, fvt, batch, seq)

        h1, meta, cnt = _out_call(o_mla, o_diff, o_fox, h, w_out[l].astype(BF16), row(ln1_g[l]),
                                  row(ln1_b[l]), rwt, rb, alpha)

        counts = cnt[:, 0].astype(jnp.int32)
        padded = ((counts + MOE_TILE - 1) // MOE_TILE) * MOE_TILE
        ends = jnp.cumsum(padded)
        starts = ends - padded
        e_idx = meta[0:2].astype(jnp.int32)
        expert_ids = jnp.arange(N_EXPERTS, dtype=jnp.int32)[:, None, None]
        rows = jnp.sum(jnp.where(e_idx[None] == expert_ids, starts[:, None, None], 0), axis=0)
        rows = rows + meta[2:4].astype(jnp.int32)
        rows = rows.T.reshape(n // ROW_TILE, 1, 2 * ROW_TILE)
        tile_start = jnp.arange(n_tiles, dtype=jnp.int32) * MOE_TILE
        tile_expert = jnp.minimum(jnp.sum(tile_start[:, None] >= ends[None, :], axis=1),
                                  N_EXPERTS - 1).astype(jnp.int32)
        n_used = (ends[-1:] // MOE_TILE).astype(jnp.int32)

        last_tile = jnp.where(padded > 0, ends - MOE_TILE, -MOE_TILE)
        fill = jnp.concatenate([(last_tile[:, None] + jnp.arange(0, MOE_TILE, ROW_TILE)[None, :]).reshape(-1),
                                ends[-1:]]).astype(jnp.int32)
        xs = _dispatch_call(fill, rows, h1, total_rows)
        ys = _moe_call(tile_expert, n_used, xs, exp_w_gate[l].astype(BF16), exp_w_up[l].astype(BF16),
                       exp_w_down[l].astype(BF16))
        h = _combine_call(rows, meta, h1, ys, row(ln2_g[l]), row(ln2_b[l]), alpha)
    return h.reshape(batch, seq, d)
```

```python
import functools
import math

import numpy as np
import jax
import jax.numpy as jnp
from jax import lax
from jax.experimental import pallas as pl
from jax.experimental.pallas import tpu as pltpu

F32 = jnp.float32
BF16 = jnp.bfloat16

D_MODEL = 1024
HEAD_DIM = 64
MLA_HEADS = 8
DIFF_HEADS = 4
FOX_HEADS = 4
MLA_Q_RANK = 384
MLA_KV_RANK = 256
MLA_NOPE = 64
MLA_ROPE = 32
DIFF_QK = 32
DIFF_ROT = 8
N_EXPERTS = 16
N_GROUPS = 4
GROUP_SIZE = 4
D_EXPERT = 512
ROPE_THETA = 500000.0
CHUNK = 64
LN_EPS = 1e-5
RMS_EPS = 1e-6
NEG_INF = -1e30
LOG2E = 1.4426950408889634

LANES = 128
SUBLANES = 8
VMEM_LIMIT_BYTES = 56 * 1024 * 1024

ROW_TILE = 256
ATTN_TILE = 256
OUT_TILE = 512
MOE_TILE = 512
KV_TILES_PER_STEP = 4
MAPS_PER_GROUP = {"mla": 1, "diff": 1, "fox": 2}
SUM_ROWS = 16
DMA_UNROLL = 16

_C_CQ, _C_CKV, _C_KR, _C_KRS = 0, 384, 640, 768
_C_DQ, _C_DQS, _C_DK, _C_DKS, _C_DV = 896, 1152, 1408, 1664, 1920
_C_FQ, _C_FK, _C_FV, _C_FF, _C_END = 2176, 2688, 3200, 3456, 3584


def _cparams(sem):
    return pltpu.CompilerParams(dimension_semantics=sem, vmem_limit_bytes=VMEM_LIMIT_BYTES)


def _split_bf16(x):
    hi = x.astype(BF16)
    return hi, (x - hi.astype(F32)).astype(BF16)


def _layer_norm(x, g, b):
    mu = jnp.mean(x, axis=-1, keepdims=True)
    xc = x - mu
    var = jnp.mean(xc * xc, axis=-1, keepdims=True)
    return xc * lax.rsqrt(var + LN_EPS) * g + b


def _rope_kernel(pos_ref, freq_ref, e128_ref, e256_ref, c128_ref, s128_ref, c256_ref, s256_ref):
    a = pos_ref[...].astype(F32) * freq_ref[...]
    cos3 = _split3_bf16(jnp.cos(a))
    sin3 = _split3_bf16(jnp.sin(a))
    c128_ref[...] = jnp.dot(cos3, e128_ref[...], preferred_element_type=F32)
    s128_ref[...] = jnp.dot(sin3, e128_ref[...], preferred_element_type=F32)
    c256_ref[...] = jnp.dot(cos3, e256_ref[...], preferred_element_type=F32)
    s256_ref[...] = jnp.dot(sin3, e256_ref[...], preferred_element_type=F32)


def _rope_call(pos, freq, e128, e256):
    n = pos.shape[0]
    spec = lambda w: pl.BlockSpec((ROW_TILE, w), lambda i: (i, 0))
    full = lambda a: pl.BlockSpec(a.shape, lambda i: (0,) * a.ndim)
    return pl.pallas_call(
        _rope_kernel, grid=(n // ROW_TILE,),
        in_specs=[spec(1), full(freq), full(e128), full(e256)],
        out_specs=[spec(128), spec(128), spec(256), spec(256)],
        out_shape=[jax.ShapeDtypeStruct((n, 128), F32), jax.ShapeDtypeStruct((n, 128), F32),
                   jax.ShapeDtypeStruct((n, 256), F32), jax.ShapeDtypeStruct((n, 256), F32)],
        compiler_params=_cparams(("parallel",)), name="rope_tables")(pos, freq, e128, e256)


def _split3_bf16(x):
    hi = x.astype(BF16)
    r1 = x - hi.astype(F32)
    mid = r1.astype(BF16)
    lo = (r1 - mid.astype(F32)).astype(BF16)
    return jnp.concatenate([hi, mid, lo], axis=1)


def _prep_kernel(q_scales, tiles_per_seq, norm_input, *refs):
    if norm_input:
        x_ref, lng_ref, lnb_ref, *refs = refs
    else:
        x_ref, *refs = refs
    (c128_ref, s128_ref, c256_ref, s256_ref, w1_ref, wuq_ref, wukv_ref, gq_ref, gkv_ref, e_ref, ones_ref,
     bff_ref, *refs) = refs
    if norm_input:
        hout_ref, *refs = refs
    mq_ref, mk_ref, mvt_ref, dq_ref, dk_ref, dvt_ref, fq_ref, fk_ref, fvt_ref, carry_ref = refs
    qs_mla, qs_diff, qs_fox = q_scales
    if norm_input:
        h = _layer_norm(x_ref[...], lng_ref[...], lnb_ref[...])
        hout_ref[...] = h
    else:
        h = x_ref[...]
    proj = jnp.dot(h.astype(BF16), w1_ref[...], preferred_element_type=F32)
    cos_m = c128_ref[...]
    sin_m = s128_ref[...]

    def rms(x, g):
        return (x * lax.rsqrt(jnp.mean(x * x, axis=-1, keepdims=True) + RMS_EPS) * g).astype(BF16)

    qe = jnp.dot(rms(proj[:, _C_CQ:_C_CKV], gq_ref[...]), wuq_ref[...], preferred_element_type=F32)
    for h in range(MLA_HEADS):
        q = qe[:, 128 * h:128 * h + 128] * cos_m + qe[:, 1024 + 128 * h:1152 + 128 * h] * sin_m
        mq_ref[0, h] = (q * qs_mla).astype(BF16)
    kve = jnp.dot(rms(proj[:, _C_CKV:_C_KR], gkv_ref[...]), wukv_ref[...], preferred_element_type=F32)
    kr = proj[:, _C_KR:_C_KRS] * cos_m + proj[:, _C_KRS:_C_DQ] * sin_m
    for h in range(MLA_HEADS):
        mk_ref[0, h] = (kve[:, 128 * h:128 * h + 128] + kr).astype(BF16)
    for p in range(MLA_HEADS // 2):
        mvt_ref[0, p, 0] = kve[:, 1024 + 128 * p:1152 + 128 * p].T.astype(BF16)

    cos_d = c256_ref[...]
    sin_d = s256_ref[...]
    dq = (proj[:, _C_DQ:_C_DQS] * cos_d + proj[:, _C_DQS:_C_DK] * sin_d) * qs_diff
    dk = proj[:, _C_DK:_C_DKS] * cos_d + proj[:, _C_DKS:_C_DV] * sin_d
    for p in range(DIFF_HEADS // 2):
        dq_ref[0, p] = dq[:, 128 * p:128 * p + 128].astype(BF16)
        dk_ref[0, p] = dk[:, 128 * p:128 * p + 128].astype(BF16)
        dvt_ref[0, p, 0] = proj[:, _C_DV + 128 * p:_C_DV + 128 * p + 128].T.astype(BF16)

    fq = proj[:, _C_FQ:_C_FK] * qs_fox + ones_ref[...]

    @pl.when(pl.program_id(0) % tiles_per_seq == 0)
    def _():
        carry_ref[...] = jnp.zeros_like(carry_ref)

    logit = proj[:, _C_FF:_C_END] + bff_ref[...]
    log_f = jnp.minimum(logit, 0.0) - jnp.log1p(jnp.exp(-jnp.abs(logit)))
    rows = log_f.shape[0]
    tri = (lax.broadcasted_iota(jnp.int32, (rows, rows), 0)
           >= lax.broadcasted_iota(jnp.int32, (rows, rows), 1)).astype(BF16)
    c3 = jnp.dot(tri, _split3_bf16(log_f), preferred_element_type=F32)
    cum = c3[:, 0:LANES] + c3[:, LANES:2 * LANES] + c3[:, 2 * LANES:3 * LANES] + carry_ref[...]
    carry_ref[...] = cum[rows - 1:rows, :]
    fkb = jnp.dot(_split3_bf16(cum * (-LOG2E)), e_ref[...], preferred_element_type=F32)
    fk = proj[:, _C_FK:_C_FV] + fkb
    for h in range(FOX_HEADS):
        fq_ref[0, h] = fq[:, 128 * h:128 * h + 128].astype(BF16)
        fk_ref[0, h] = fk[:, 128 * h:128 * h + 128].astype(BF16)
    for p in range(FOX_HEADS // 2):
        fvt_ref[0, p, 0] = proj[:, _C_FV + 128 * p:_C_FV + 128 * p + 128].T.astype(BF16)


def _prep_call(h, tables, w1, wuq, wukv, gq, gkv, emat, ones, b_ff, batch, seq, q_scales, ln_in=None):
    tm = ROW_TILE
    nt = seq // tm
    c128, s128, c256, s256 = tables
    row = lambda w: pl.BlockSpec((tm, w), lambda i: (i, 0))
    full = lambda a: pl.BlockSpec(a.shape, lambda i: (0,) * a.ndim)
    head = lambda nh: pl.BlockSpec((1, nh, tm, 128), lambda i: (i // nt, 0, i % nt, 0))
    vt = lambda npair: pl.BlockSpec((1, npair, 1, 128, tm), lambda i: (i // nt, 0, i % nt, 0, 0))
    hshape = lambda nh: jax.ShapeDtypeStruct((batch, nh, seq, 128), BF16)
    vshape = lambda npair: jax.ShapeDtypeStruct((batch, npair, nt, 128, tm), BF16)
    norm_input = ln_in is not None
    ln_args = list(ln_in) if norm_input else []
    weights = [w1, wuq, wukv, gq, gkv, emat, ones, b_ff]
    return pl.pallas_call(
        functools.partial(_prep_kernel, q_scales, nt, norm_input), grid=(batch * nt,),
        in_specs=([row(D_MODEL)] + [full(a) for a in ln_args] + [row(128), row(128), row(256), row(256)]
                  + [full(a) for a in weights]),
        out_specs=([row(D_MODEL)] if norm_input else []) + [
            head(8), head(8), vt(4), head(2), head(2), vt(2), head(4), head(4), vt(2)],
        out_shape=([jax.ShapeDtypeStruct(h.shape, F32)] if norm_input else []) + [
            hshape(8), hshape(8), vshape(4), hshape(2), hshape(2), vshape(2), hshape(4), hshape(4), vshape(2)],
        scratch_shapes=[pltpu.VMEM((1, LANES), F32)],
        compiler_params=_cparams(("arbitrary",)), name="prep")(
            h, *ln_args, c128, s128, c256, s256, *weights)


def _attn_kernel(kind, lam_init, *refs):
    if kind == "diff":
        lam_ref, q_ref, k_ref, vt_ref, g_ref, o_ref, m_ref, acc_ref, s_ref, bm_ref = refs
    else:
        q_ref, k_ref, vt_ref, o_ref, m_ref, acc_ref, s_ref, bm_ref = refs
    t = ATTN_TILE
    qi = pl.program_id(1)
    npair = vt_ref.shape[1]

    qs, k_of_map, v_of_map = [], [], []
    if kind == "diff":
        for p in range(npair):
            q2 = q_ref[0, p]
            lane = lax.broadcasted_iota(jnp.int32, q2.shape, 1)
            for sub in range(4):
                keep = (lane >= DIFF_QK * sub) & (lane < DIFF_QK * (sub + 1))
                qs.append(jnp.where(keep, q2, jnp.zeros_like(q2)))
                k_of_map.append(p)
                v_of_map.append((p, sub // 2))
    else:
        for h in range(2 * npair):
            qs.append(q_ref[0, h])
            k_of_map.append(h)
            v_of_map.append((h // 2, h % 2))
    nmap = len(qs)

    m_ref[...] = jnp.full(m_ref.shape, NEG_INF, F32)
    acc_ref[...] = jnp.zeros(acc_ref.shape, F32)

    kk = lax.broadcasted_iota(jnp.int32, (t, t), 0)
    qq = lax.broadcasted_iota(jnp.int32, (t, t), 1)
    if kind == "fox":
        diag_mask = kk <= qq
    else:
        diag_mask = (kk // CHUNK) <= (qq // CHUNK)
    ones_rows = jnp.ones((SUM_ROWS, t), BF16)

    def score_block(j, slot, maps=None):
        start = pl.multiple_of(j * t, t)
        for m in (range(nmap) if maps is None else maps):
            kblk = k_ref[0, k_of_map[m], pl.ds(start, t), :]
            s = lax.dot_general(kblk, qs[m], (((1,), (1,)), ((), ())), preferred_element_type=F32)
            s_ref[slot, m] = s
            bm_ref[slot, m] = jnp.max(s, axis=0, keepdims=True)

    def softmax_block(j, slot, masked, maps=None):
        maps = list(range(nmap)) if maps is None else list(maps)
        probs, alphas = {}, {}
        for m in maps:
            s = s_ref[slot, m]
            m_old = m_ref[m]
            if masked:
                s = jnp.where(diag_mask, s, NEG_INF)
                m_new = jnp.maximum(m_old, jnp.max(s, axis=0, keepdims=True))
            else:
                m_new = jnp.maximum(m_old, bm_ref[slot, m])
            alphas[m] = jnp.exp2(m_old - m_new)
            probs[m] = jnp.exp2(s - m_new).astype(BF16)
            m_ref[m] = m_new
        for m in maps:
            pair, hd = v_of_map[m]
            vblk = jnp.concatenate([vt_ref[0, pair, j, HEAD_DIM * hd:HEAD_DIM * (hd + 1), :], ones_rows],
                                   axis=0)
            pv = jnp.dot(vblk, probs[m], preferred_element_type=F32)
            acc_ref[m] = alphas[m] * acc_ref[m] + pv

    score_block(0, 0)
    unroll = KV_TILES_PER_STEP
    group = MAPS_PER_GROUP[kind]

    def pipelined_tile(j, u):
        for g in range(0, nmap, group):
            part = range(g, g + group)
            score_block(j + 1, (u + 1) % 2, part)
            softmax_block(j, u % 2, False, part)

    def tiles(jj, carry):
        j = unroll * jj
        for u in range(unroll):
            pipelined_tile(j + u, u)
        return carry

    lax.fori_loop(0, qi // unroll, tiles, 0)
    base = qi - qi % unroll
    for rem in range(unroll):
        @pl.when(qi % unroll == rem)
        def _():
            for u in range(rem):
                pipelined_tile(base + u, u)
            softmax_block(qi, rem % 2, True)

    def normalized(m):
        return acc_ref[m, 0:HEAD_DIM, :] / acc_ref[m, HEAD_DIM:HEAD_DIM + 1, :]

    for p in range(npair):
        if kind == "diff":
            lam = lam_ref[0]
            outs = []
            for hd in range(2):
                o = normalized(4 * p + 2 * hd) - lam * normalized(4 * p + 2 * hd + 1)
                outs.append(o * lax.rsqrt(jnp.mean(o * o, axis=0, keepdims=True) + RMS_EPS))
            ot = jnp.concatenate(outs, axis=0).T
            ot = ot * g_ref[...] * (1.0 - lam_init)
        else:
            ot = jnp.concatenate([normalized(2 * p), normalized(2 * p + 1)], axis=0).T
        o_ref[:, 128 * p:128 * (p + 1)] = ot.astype(BF16)


def _attn_call(kind, q, k, vt, batch, seq, lam=None, lam_init=0.0, subln_g=None):
    t = ATTN_TILE
    nq = seq // t
    npair = vt.shape[1]
    nslab = q.shape[1]
    nmap = 4 * npair if kind == "diff" else 2 * npair
    resident = pl.Buffered(1)
    in_specs = [pl.BlockSpec((1, nslab, t, 128), lambda b, i: (b, 0, i, 0)),
                pl.BlockSpec((1, nslab, seq, 128), lambda b, i: (b, 0, 0, 0), pipeline_mode=resident),
                pl.BlockSpec((1, npair, nq, 128, t), lambda b, i: (b, 0, 0, 0, 0), pipeline_mode=resident)]
    args = [q, k, vt]
    if kind == "diff":
        in_specs = [pl.BlockSpec(memory_space=pltpu.SMEM)] + in_specs + [
            pl.BlockSpec((1, 128), lambda b, i: (0, 0))]
        args = [lam] + args + [subln_g]
    return pl.pallas_call(
        functools.partial(_attn_kernel, kind, lam_init), grid=(batch, nq),
        in_specs=in_specs,
        out_specs=pl.BlockSpec((t, 128 * npair), lambda b, i: (b * nq + i, 0)),
        out_shape=jax.ShapeDtypeStruct((batch * seq, 128 * npair), BF16),
        scratch_shapes=[pltpu.VMEM((nmap, 1, t), F32),
                        pltpu.VMEM((nmap, HEAD_DIM + SUM_ROWS, t), F32),
                        pltpu.VMEM((2, nmap, t, t), F32), pltpu.VMEM((2, nmap, 1, t), F32)],
        compiler_params=_cparams(("parallel", "arbitrary")),
        name="attn_" + kind)(*args)


def _out_kernel(alpha, om_ref, od_ref, of_ref, h_ref, wo_ref, g_ref, b_ref, rhi_ref, rlo_ref, rb_ref,
                h1_ref, meta_ref, cnt_out_ref, cnt_ref, sel_ref):
    @pl.when(pl.program_id(0) == 0)
    def _():
        cnt_ref[...] = jnp.zeros_like(cnt_ref)

    mix = (jnp.dot(om_ref[...], wo_ref[0:512, :], preferred_element_type=F32)
           + jnp.dot(od_ref[...], wo_ref[512:768, :], preferred_element_type=F32)
           + jnp.dot(of_ref[...], wo_ref[768:1024, :], preferred_element_type=F32))
    h1 = _layer_norm(alpha * h_ref[...] + mix, g_ref[...], b_ref[...])
    h1_ref[...] = h1

    h_hi, h_lo = _split_bf16(h1)
    nt = lambda x, y: lax.dot_general(x, y, (((1,), (1,)), ((), ())), preferred_element_type=F32)
    logits = nt(rhi_ref[...], h_hi) + nt(rhi_ref[...], h_lo) + nt(rlo_ref[...], h_hi)
    score = jax.nn.sigmoid(logits)
    biased = score + rb_ref[...]
    b = [biased[e:e + 1, :] for e in range(N_EXPERTS)]
    sc = [score[e:e + 1, :] for e in range(N_EXPERTS)]

    gscore = []
    for g in range(N_GROUPS):
        x = b[GROUP_SIZE * g:GROUP_SIZE * (g + 1)]
        pair_sums = [x[i] + x[j] for i in range(GROUP_SIZE) for j in range(i + 1, GROUP_SIZE)]
        gscore.append(functools.reduce(jnp.maximum, pair_sums))
    sel = []
    for g in range(N_GROUPS):
        first_max = None
        for g2 in range(N_GROUPS):
            if g2 == g:
                continue
            c = (gscore[g] > gscore[g2]) if g2 < g else (gscore[g] >= gscore[g2])
            first_max = c if first_max is None else (first_max & c)
        x = b[GROUP_SIZE * g:GROUP_SIZE * (g + 1)]
        for j in range(GROUP_SIZE):
            ahead = jnp.zeros_like(x[j])
            for i in range(GROUP_SIZE):
                if i < j:
                    ahead = ahead + (x[i] >= x[j]).astype(F32)
                elif i > j:
                    ahead = ahead + (x[i] > x[j]).astype(F32)
            sel.append(first_max & (ahead < 2.0))

    zero = jnp.zeros_like(sc[0])
    denom = zero
    e_lo = jnp.full_like(zero, 99.0)
    e_hi = jnp.full_like(zero, -1.0)
    for e in range(N_EXPERTS):
        sel_ref[e:e + 1, :] = sel[e].astype(F32)
        denom = denom + jnp.where(sel[e], sc[e], 0.0)
        e_lo = jnp.minimum(e_lo, jnp.where(sel[e], float(e), 99.0))
        e_hi = jnp.maximum(e_hi, jnp.where(sel[e], float(e), -1.0))

    selm = sel_ref[...]
    rows = selm.shape[1]
    r = lax.broadcasted_iota(jnp.int32, (rows, rows), 0)
    c = lax.broadcasted_iota(jnp.int32, (rows, rows), 1)
    before = (r < c).astype(BF16)
    rank = jnp.dot(selm.astype(BF16), before, preferred_element_type=F32) + cnt_ref[:, 0:1]
    g_lo, g_hi, r_lo, r_hi = zero, zero, zero, zero
    for e in range(N_EXPERTS):
        gate = sc[e] / denom
        is_lo = e_lo == float(e)
        is_hi = e_hi == float(e)
        g_lo = g_lo + jnp.where(is_lo, gate, 0.0)
        g_hi = g_hi + jnp.where(is_hi, gate, 0.0)
        r_lo = r_lo + jnp.where(is_lo, rank[e:e + 1, :], 0.0)
        r_hi = r_hi + jnp.where(is_hi, rank[e:e + 1, :], 0.0)
    for i, v in enumerate((e_lo, e_hi, r_lo, r_hi, g_lo, g_hi, zero, zero)):
        meta_ref[i:i + 1, :] = v
    cnt_ref[...] = cnt_ref[...] + jnp.sum(selm, axis=1, keepdims=True)
    cnt_out_ref[...] = cnt_ref[...]


def _out_call(om, od, of, h, wo, g, b, rwt, rb, alpha):
    r_hi, r_lo = _split_bf16(rwt)
    n = h.shape[0]
    tm = OUT_TILE
    row = lambda w: pl.BlockSpec((tm, w), lambda i: (i, 0))
    full = lambda a: pl.BlockSpec(a.shape, lambda i: (0,) * a.ndim)
    return pl.pallas_call(
        functools.partial(_out_kernel, alpha), grid=(n // tm,),
        in_specs=[row(512), row(256), row(256), row(D_MODEL), full(wo), full(g), full(b),
                  full(r_hi), full(r_lo), full(rb)],
        out_specs=[row(D_MODEL), pl.BlockSpec((SUBLANES, tm), lambda i: (0, i)),
                   pl.BlockSpec((N_EXPERTS, LANES), lambda i: (0, 0))],
        out_shape=[jax.ShapeDtypeStruct((n, D_MODEL), F32), jax.ShapeDtypeStruct((SUBLANES, n), F32),
                   jax.ShapeDtypeStruct((N_EXPERTS, LANES), F32)],
        scratch_shapes=[pltpu.VMEM((N_EXPERTS, LANES), F32), pltpu.VMEM((N_EXPERTS, tm), F32)],
        compiler_params=_cparams(("arbitrary",)), name="outproj_router")(
            om, od, of, h, wo, g, b, r_hi, r_lo, rb)


def _row_copy(src_ref, src_row, dst_ref, dst_row, sem):
    return pltpu.make_async_copy(src_ref.at[pl.ds(src_row, 1)], dst_ref.at[pl.ds(dst_row, 1)], sem)


def _dispatch_kernel(fill_ref, rows_ref, h_ref, xs_ref, zero_ref, sem):
    tm = h_ref.shape[0]

    @pl.when(pl.program_id(0) == 0)
    def _():
        zero_ref[...] = jnp.zeros_like(zero_ref)
        n_fixed = fill_ref.shape[0] - 1
        tail = fill_ref[n_fixed]
        n_tail = (xs_ref.shape[0] - tail) // tm

        def chunk_copy(row):
            return pltpu.make_async_copy(zero_ref, xs_ref.at[pl.ds(pl.multiple_of(row, tm), tm)], sem)

        def chunk_row(c):
            return jnp.where(c < n_fixed, fill_ref[jnp.minimum(c, n_fixed - 1)], tail + (c - n_fixed) * tm)

        def fill(c, carry):
            row = chunk_row(c)

            @pl.when(row >= 0)
            def _():
                chunk_copy(row).start()
            return carry

        def fill_wait(c, carry):
            row = chunk_row(c)

            @pl.when(row >= 0)
            def _():
                chunk_copy(row).wait()
            return carry

        lax.fori_loop(0, n_fixed + n_tail, fill, 0)
        lax.fori_loop(0, n_fixed + n_tail, fill_wait, 0)

    def start(c, carry):
        for u in range(DMA_UNROLL):
            r = c * DMA_UNROLL + u
            for s in range(2):
                _row_copy(h_ref, r, xs_ref, rows_ref[0, 0, 2 * r + s], sem).start(priority=s)
        return carry

    def wait(c, carry):
        for _ in range(2 * DMA_UNROLL):
            _row_copy(h_ref, 0, xs_ref, 0, sem).wait()
        return carry

    lax.fori_loop(0, tm // DMA_UNROLL, start, 0)
    lax.fori_loop(0, tm // DMA_UNROLL, wait, 0)


def _dispatch_call(fill, rows, h1, total_rows):
    n, d = h1.shape
    tm = ROW_TILE
    grid_spec = pltpu.PrefetchScalarGridSpec(
        num_scalar_prefetch=1, grid=(n // tm,),
        in_specs=[pl.BlockSpec((1, 1, 2 * tm), lambda i, f: (i, 0, 0), memory_space=pltpu.SMEM),
                  pl.BlockSpec((tm, d), lambda i, f: (i, 0))],
        out_specs=pl.BlockSpec(memory_space=pl.ANY),
        scratch_shapes=[pltpu.VMEM((tm, d), F32), pltpu.SemaphoreType.DMA(())])
    return pl.pallas_call(
        _dispatch_kernel, grid_spec=grid_spec,
        out_shape=jax.ShapeDtypeStruct((total_rows, d), F32),
        compiler_params=_cparams(("arbitrary",)), name="moe_dispatch")(fill, rows, h1)


def _moe_kernel(te_ref, nu_ref, x_ref, wg_ref, wu_ref, wd_ref, y_ref):
    del te_ref
    i = pl.program_id(0)

    @pl.when(i < nu_ref[0])
    def _():
        x = x_ref[...].astype(BF16)
        gate = jnp.dot(x, wg_ref[0], preferred_element_type=F32)
        up = jnp.dot(x, wu_ref[0], preferred_element_type=F32)
        a = (gate * jax.nn.sigmoid(gate) * up).astype(BF16)
        y_ref[...] = jnp.dot(a, wd_ref[0], preferred_element_type=F32)

    @pl.when(i >= nu_ref[0])
    def _():
        y_ref[...] = jnp.zeros_like(y_ref)


def _moe_call(tile_expert, n_used, xs, wg, wu, wd):
    total_rows, d = xs.shape
    tm = MOE_TILE
    grid_spec = pltpu.PrefetchScalarGridSpec(
        num_scalar_prefetch=2, grid=(total_rows // tm,),
        in_specs=[pl.BlockSpec((tm, d), lambda i, te, nu: (jnp.minimum(i, nu[0] - 1), 0)),
                  pl.BlockSpec((1, d, D_EXPERT), lambda i, te, nu: (te[i], 0, 0)),
                  pl.BlockSpec((1, d, D_EXPERT), lambda i, te, nu: (te[i], 0, 0)),
                  pl.BlockSpec((1, D_EXPERT, d), lambda i, te, nu: (te[i], 0, 0))],
        out_specs=pl.BlockSpec((tm, d), lambda i, te, nu: (i, 0)))
    return pl.pallas_call(
        _moe_kernel, grid_spec=grid_spec,
        out_shape=jax.ShapeDtypeStruct((total_rows, d), F32),
        compiler_params=_cparams(("arbitrary",)), name="moe_experts")(
            tile_expert, n_used, xs, wg, wu, wd)


def _combine_kernel(alpha, rows_ref, meta_ref, h1_ref, ys_ref, g_ref, b_ref, o_ref, ybuf_ref, sem):
    tm = h1_ref.shape[0]

    def start(c, carry):
        for u in range(DMA_UNROLL):
            r = c * DMA_UNROLL + u
            for s in range(2):
                _row_copy(ys_ref, rows_ref[0, 0, 2 * r + s], ybuf_ref.at[s], r, sem).start(priority=s)
        return carry

    def wait(c, carry):
        for _ in range(2 * DMA_UNROLL):
            _row_copy(ys_ref, 0, ybuf_ref.at[0], 0, sem).wait()
        return carry

    lax.fori_loop(0, tm // DMA_UNROLL, start, 0)
    meta_rows = jnp.concatenate([meta_ref[...], jnp.zeros((LANES - SUBLANES, tm), F32)], axis=0)
    gcol = meta_rows.T
    lax.fori_loop(0, tm // DMA_UNROLL, wait, 0)
    ffn = gcol[:, 4:5] * ybuf_ref[0] + gcol[:, 5:6] * ybuf_ref[1]
    o_ref[...] = _layer_norm(alpha * h1_ref[...] + ffn, g_ref[...], b_ref[...])


def _combine_call(rows, meta, h1, ys, g, b, alpha):
    n, d = h1.shape
    tm = ROW_TILE
    vec = pl.BlockSpec((1, d), lambda i: (0, 0))
    return pl.pallas_call(
        functools.partial(_combine_kernel, alpha), grid=(n // tm,),
        in_specs=[pl.BlockSpec((1, 1, 2 * tm), lambda i: (i, 0, 0), memory_space=pltpu.SMEM),
                  pl.BlockSpec((SUBLANES, tm), lambda i: (0, i)),
                  pl.BlockSpec((tm, d), lambda i: (i, 0)),
                  pl.BlockSpec(memory_space=pl.ANY), vec, vec],
        out_specs=pl.BlockSpec((tm, d), lambda i: (i, 0)),
        out_shape=jax.ShapeDtypeStruct((n, d), F32),
        scratch_shapes=[pltpu.VMEM((2, tm, d), F32), pltpu.SemaphoreType.DMA(())],
        compiler_params=_cparams(("arbitrary",)), name="moe_combine")(rows, meta, h1, ys, g, b)


def _widen_w_in(w_in):
    z = lambda c: jnp.zeros((D_MODEL, c), F32)
    c_q, c_kv = w_in[:, 0:384], w_in[:, 384:640]
    kr = w_in[:, 640:672]
    dq, dk, dv = w_in[:, 672:928], w_in[:, 928:1184], w_in[:, 1184:1440]
    fq, fk, fv = w_in[:, 1440:1696], w_in[:, 1696:1952], w_in[:, 1952:2208]
    kr_blk = jnp.concatenate([z(64), kr, z(32)], axis=1)
    kr_swp = jnp.concatenate([z(64), -kr[:, 16:32], kr[:, 0:16], z(32)], axis=1)

    def diff_swap(w):
        w4 = w.reshape(D_MODEL, 8, DIFF_QK)
        half = DIFF_ROT // 2
        sw = jnp.concatenate([-w4[:, :, half:DIFF_ROT], w4[:, :, 0:half],
                              jnp.zeros((D_MODEL, 8, DIFF_QK - DIFF_ROT), F32)], axis=2)
        return sw.reshape(D_MODEL, 256)

    def pad_heads(w):
        w4 = w.reshape(D_MODEL, FOX_HEADS, HEAD_DIM)
        return jnp.concatenate([w4, jnp.zeros_like(w4)], axis=2).reshape(D_MODEL, FOX_HEADS * 128)

    w_ff = jnp.concatenate([w_in[:, 2208:2212], z(LANES - FOX_HEADS)], axis=1)
    w1 = jnp.concatenate([c_q, c_kv, kr_blk, kr_swp, dq, diff_swap(dq), dk, diff_swap(dk), dv,
                          pad_heads(fq), pad_heads(fk), fv, w_ff], axis=1)
    assert w1.shape[1] == _C_END
    return w1.astype(BF16)


def _widen_mla(w_uq, w_ukv):
    wq = w_uq.reshape(MLA_Q_RANK, MLA_HEADS, MLA_NOPE + MLA_ROPE)
    zq = jnp.zeros((MLA_Q_RANK, MLA_HEADS, 32), F32)
    half = MLA_ROPE // 2
    q_main = jnp.concatenate([wq, zq], axis=2)
    q_swap = jnp.concatenate([jnp.zeros((MLA_Q_RANK, MLA_HEADS, MLA_NOPE), F32),
                              -wq[:, :, MLA_NOPE + half:], wq[:, :, MLA_NOPE:MLA_NOPE + half], zq], axis=2)
    wuq = jnp.concatenate([q_main.reshape(MLA_Q_RANK, 1024), q_swap.reshape(MLA_Q_RANK, 1024)], axis=1)
    wkv = w_ukv.reshape(MLA_KV_RANK, MLA_HEADS, 128)
    k_part = jnp.concatenate([wkv[:, :, :MLA_NOPE], jnp.zeros((MLA_KV_RANK, MLA_HEADS, 64), F32)], axis=2)
    v_part = wkv[:, :, MLA_NOPE:]
    wukv = jnp.concatenate([k_part.reshape(MLA_KV_RANK, 1024), v_part.reshape(MLA_KV_RANK, 512)], axis=1)
    return wuq.astype(BF16), wukv.astype(BF16)


def _fox_lane_constants():
    ones = np.zeros((1, FOX_HEADS * 128), np.float32)
    emat = np.zeros((3 * LANES, FOX_HEADS * 128), np.float32)
    for h in range(FOX_HEADS):
        for part in range(3):
            ones[0, 128 * h + HEAD_DIM + part] = 1.0
            emat[LANES * part + h, 128 * h + HEAD_DIM + part] = 1.0
    return jnp.asarray(ones), jnp.asarray(emat, dtype=BF16)


def _rope_lane_constants():
    n_m, n_d = MLA_ROPE // 2, DIFF_ROT // 2
    inv_m = ROPE_THETA ** (-jnp.arange(0, MLA_ROPE, 2, dtype=F32) / MLA_ROPE)
    inv_d = ROPE_THETA ** (-jnp.arange(0, DIFF_ROT, 2, dtype=F32) / DIFF_ROT)
    freq = jnp.concatenate([inv_m, inv_d, jnp.zeros((LANES - n_m - n_d,), F32)]).reshape(1, LANES)
    src128 = np.full((128,), LANES - 1)
    src128[MLA_NOPE:MLA_NOPE + n_m] = np.arange(n_m)
    src128[MLA_NOPE + n_m:MLA_NOPE + 2 * n_m] = np.arange(n_m)
    src32 = np.full((DIFF_QK,), LANES - 1)
    src32[0:n_d] = n_m + np.arange(n_d)
    src32[n_d:2 * n_d] = n_m + np.arange(n_d)
    src256 = np.tile(src32, 8)

    def copy_matrix(src):
        e = np.zeros((LANES, src.shape[0]), np.float32)
        e[src, np.arange(src.shape[0])] = 1.0
        return jnp.asarray(np.concatenate([e, e, e], axis=0), dtype=BF16)

    return freq, copy_matrix(src128), copy_matrix(src256)


def kernel(x, positions, ln_in_g, ln_in_b, w_in, mla_q_norm_g, mla_kv_norm_g, mla_w_uq, mla_w_ukv,
           diff_lam_q1, diff_lam_k1, diff_lam_q2, diff_lam_k2, diff_subln_g, fox_f_bias, w_out,
           ln1_g, ln1_b, router_w, router_bias, exp_w_gate, exp_w_up, exp_w_down, ln2_g, ln2_b):
    batch, seq, d = x.shape
    depth = w_in.shape[0]
    n = batch * seq
    alpha = (2 * depth) ** 0.25
    total_rows = 2 * n + N_EXPERTS * MOE_TILE
    n_tiles = total_rows // MOE_TILE
    q_scales = ((MLA_NOPE + MLA_ROPE) ** -0.5 * LOG2E, DIFF_QK ** -0.5 * LOG2E, HEAD_DIM ** -0.5 * LOG2E)

    row = lambda v: v.reshape(1, -1).astype(F32)
    tables = _rope_call(positions.reshape(n, 1).astype(jnp.int32), *_rope_lane_constants())
    ones_fox, emat = _fox_lane_constants()
    rwt = router_w.T.astype(F32)
    rb = router_bias.reshape(N_EXPERTS, 1).astype(F32)

    h = x.reshape(n, d).astype(F32)
    for l in range(depth):
        lam_init = 0.8 - 0.6 * math.exp(-0.3 * l)
        lam = (jnp.exp(jnp.sum(diff_lam_q1[l].astype(F32) * diff_lam_k1[l].astype(F32)))
               - jnp.exp(jnp.sum(diff_lam_q2[l].astype(F32) * diff_lam_k2[l].astype(F32))) + lam_init)
        w1 = _widen_w_in(w_in[l])
        wuq, wukv = _widen_mla(mla_w_uq[l], mla_w_ukv[l])
        b_ff = jnp.concatenate([fox_f_bias[l].astype(F32), jnp.zeros((LANES - FOX_HEADS,), F32)]).reshape(1, LANES)

        prep_out = _prep_call(h, tables, w1, wuq, wukv, row(mla_q_norm_g[l]), row(mla_kv_norm_g[l]),
                              emat, ones_fox, b_ff, batch, seq, q_scales,
                              ln_in=(row(ln_in_g), row(ln_in_b)) if l == 0 else None)
        if l == 0:
            h, *prep_out = prep_out
        mq, mk, mvt, dq, dk, dvt, fq, fk, fvt = prep_out
        o_mla = _attn_call("mla", mq, mk, mvt, batch, seq)
        o_diff = _attn_call("diff", dq, dk, dvt, batch, seq, lam=lam.reshape(1).astype(F32),
                            lam_init=lam_init, subln_g=row(jnp.tile(diff_subln_g[l], 2)))
        o_fox = _attn_call("fox", fq, fk, fvt, batch, seq)

        h1, meta, cnt = _out_call(o_mla, o_diff, o_fox, h, w_out[l].astype(BF16), row(ln1_g[l]),
                                  row(ln1_b[l]), rwt, rb, alpha)

        counts = cnt[:, 0].astype(jnp.int32)
        padded = ((counts + MOE_TILE - 1) // MOE_TILE) * MOE_TILE
        ends = jnp.cumsum(padded)
        starts = ends - padded
        e_idx = meta[0:2].astype(jnp.int32)
        expert_ids = jnp.arange(N_EXPERTS, dtype=jnp.int32)[:, None, None]
        rows = jnp.sum(jnp.where(e_idx[None] == expert_ids, starts[:, None, None], 0), axis=0)
        rows = rows + meta[2:4].astype(jnp.int32)
        rows = rows.T.reshape(n // ROW_TILE, 1, 2 * ROW_TILE)
        tile_start = jnp.arange(n_tiles, dtype=jnp.int32) * MOE_TILE
        tile_expert = jnp.minimum(jnp.sum(tile_start[:, None] >= ends[None, :], axis=1),
                                  N_EXPERTS - 1).astype(jnp.int32)
        n_used = (ends[-1:] // MOE_TILE).astype(jnp.int32)

        last_tile = jnp.where(padded > 0, ends - MOE_TILE, -MOE_TILE)
        fill = jnp.concatenate([(last_tile[:, None] + jnp.arange(0, MOE_TILE, ROW_TILE)[None, :]).reshape(-1),
                                ends[-1:]]).astype(jnp.int32)
        xs = _dispatch_call(fill, rows, h1, total_rows)
        ys = _moe_call(tile_expert, n_used, xs, exp_w_gate[l].astype(BF16), exp_w_up[l].astype(BF16),
                       exp_w_down[l].astype(BF16))
        h = _combine_call(rows, meta, h1, ys, row(ln2_g[l]), row(ln2_b[l]), alpha)
    return h.reshape(batch, seq, d)
```

```python
import functools
import math

import numpy as np
import jax
import jax.numpy as jnp
from jax import lax
from jax.experimental import pallas as pl
from jax.experimental.pallas import tpu as pltpu

F32 = jnp.float32
BF16 = jnp.bfloat16

D_MODEL = 1024
HEAD_DIM = 64
MLA_HEADS = 8
DIFF_HEADS = 4
FOX_HEADS = 4
MLA_Q_RANK = 384
MLA_KV_RANK = 256
MLA_NOPE = 64
MLA_ROPE = 32
DIFF_QK = 32
DIFF_ROT = 8
N_EXPERTS = 16
N_GROUPS = 4
GROUP_SIZE = 4
D_EXPERT = 512
ROPE_THETA = 500000.0
CHUNK = 64
LN_EPS = 1e-5
RMS_EPS = 1e-6
NEG_INF = -1e30
LOG2E = 1.4426950408889634

LANES = 128
SUBLANES = 8
VMEM_LIMIT_BYTES = 56 * 1024 * 1024

ROW_TILE = 256
ATTN_TILE = 256
OUT_TILE = 512
MOE_TILE = 512
KV_TILES_PER_STEP = 4
MAPS_PER_GROUP = {"mla": 1, "diff": 1, "fox": 2}
SUM_ROWS = 16
DMA_UNROLL = 16

_C_CQ, _C_CKV, _C_KR, _C_KRS = 0, 384, 640, 768
_C_DQ, _C_DQS, _C_DK, _C_DKS, _C_DV = 896, 1152, 1408, 1664, 1920
_C_FQ, _C_FK, _C_FV, _C_FF, _C_END = 2176, 2688, 3200, 3456, 3584


def _cparams(sem):
    return pltpu.CompilerParams(dimension_semantics=sem, vmem_limit_bytes=VMEM_LIMIT_BYTES)


def _split_bf16(x):
    hi = x.astype(BF16)
    return hi, (x - hi.astype(F32)).astype(BF16)


def _layer_norm(x, g, b):
    mu = jnp.mean(x, axis=-1, keepdims=True)
    xc = x - mu
    var = jnp.mean(xc * xc, axis=-1, keepdims=True)
    return xc * lax.rsqrt(var + LN_EPS) * g + b


def _rope_kernel(pos_ref, freq_ref, e128_ref, e256_ref, c128_ref, s128_ref, c256_ref, s256_ref):
    a = pos_ref[...].astype(F32) * freq_ref[...]
    cos3 = _split3_bf16(jnp.cos(a))
    sin3 = _split3_bf16(jnp.sin(a))
    c128_ref[...] = jnp.dot(cos3, e128_ref[...], preferred_element_type=F32)
    s128_ref[...] = jnp.dot(sin3, e128_ref[...], preferred_element_type=F32)
    c256_ref[...] = jnp.dot(cos3, e256_ref[...], preferred_element_type=F32)
    s256_ref[...] = jnp.dot(sin3, e256_ref[...], preferred_element_type=F32)


def _rope_call(pos, freq, e128, e256):
    n = pos.shape[0]
    spec = lambda w: pl.BlockSpec((ROW_TILE, w), lambda i: (i, 0))
    full = lambda a: pl.BlockSpec(a.shape, lambda i: (0,) * a.ndim)
    return pl.pallas_call(
        _rope_kernel, grid=(n // ROW_TILE,),
        in_specs=[spec(1), full(freq), full(e128), full(e256)],
        out_specs=[spec(128), spec(128), spec(256), spec(256)],
        out_shape=[jax.ShapeDtypeStruct((n, 128), F32), jax.ShapeDtypeStruct((n, 128), F32),
                   jax.ShapeDtypeStruct((n, 256), F32), jax.ShapeDtypeStruct((n, 256), F32)],
        compiler_params=_cparams(("parallel",)), name="rope_tables")(pos, freq, e128, e256)


def _split3_bf16(x):
    hi = x.astype(BF16)
    r1 = x - hi.astype(F32)
    mid = r1.astype(BF16)
    lo = (r1 - mid.astype(F32)).astype(BF16)
    return jnp.concatenate([hi, mid, lo], axis=1)


def _prep_kernel(q_scales, tiles_per_seq, norm_input, *refs):
    if norm_input:
        x_ref, lng_ref, lnb_ref, *refs = refs
    else:
        x_ref, *refs = refs
    (c128_ref, s128_ref, c256_ref, s256_ref, w1_ref, wuq_ref, wukv_ref, gq_ref, gkv_ref, e_ref, ones_ref,
     bff_ref, *refs) = refs
    if norm_input:
        hout_ref, *refs = refs
    mq_ref, mk_ref, mvt_ref, dq_ref, dk_ref, dvt_ref, fq_ref, fk_ref, fvt_ref, carry_ref = refs
    qs_mla, qs_diff, qs_fox = q_scales
    if norm_input:
        h = _layer_norm(x_ref[...], lng_ref[...], lnb_ref[...])
        hout_ref[...] = h
    else:
        h = x_ref[...]
    proj = jnp.dot(h.astype(BF16), w1_ref[...], preferred_element_type=F32)
    cos_m = c128_ref[...]
    sin_m = s128_ref[...]

    def rms(x, g):
        return (x * lax.rsqrt(jnp.mean(x * x, axis=-1, keepdims=True) + RMS_EPS) * g).astype(BF16)

    qe = jnp.dot(rms(proj[:, _C_CQ:_C_CKV], gq_ref[...]), wuq_ref[...], preferred_element_type=F32)
    for h in range(MLA_HEADS):
        q = qe[:, 128 * h:128 * h + 128] * cos_m + qe[:, 1024 + 128 * h:1152 + 128 * h] * sin_m
        mq_ref[0, h] = (q * qs_mla).astype(BF16)
    kve = jnp.dot(rms(proj[:, _C_CKV:_C_KR], gkv_ref[...]), wukv_ref[...], preferred_element_type=F32)
    kr = proj[:, _C_KR:_C_KRS] * cos_m + proj[:, _C_KRS:_C_DQ] * sin_m
    for h in range(MLA_HEADS):
        mk_ref[0, h] = (kve[:, 128 * h:128 * h + 128] + kr).astype(BF16)
    for p in range(MLA_HEADS // 2):
        mvt_ref[0, p, 0] = kve[:, 1024 + 128 * p:1152 + 128 * p].T.astype(BF16)

    cos_d = c256_ref[...]
    sin_d = s256_ref[...]
    dq = (proj[:, _C_DQ:_C_DQS] * cos_d + proj[:, _C_DQS:_C_DK] * sin_d) * qs_diff
    dk = proj[:, _C_DK:_C_DKS] * cos_d + proj[:, _C_DKS:_C_DV] * sin_d
    for p in range(DIFF_HEADS // 2):
        dq_ref[0, p] = dq[:, 128 * p:128 * p + 128].astype(BF16)
        dk_ref[0, p] = dk[:, 128 * p:128 * p + 128].astype(BF16)
        dvt_ref[0, p, 0] = proj[:, _C_DV + 128 * p:_C_DV + 128 * p + 128].T.astype(BF16)

    fq = proj[:, _C_FQ:_C_FK] * qs_fox + ones_ref[...]

    @pl.when(pl.program_id(0) % tiles_per_seq == 0)
    def _():
        carry_ref[...] = jnp.zeros_like(carry_ref)

    logit = proj[:, _C_FF:_C_END] + bff_ref[...]
    log_f = jnp.minimum(logit, 0.0) - jnp.log1p(jnp.exp(-jnp.abs(logit)))
    rows = log_f.shape[0]
    tri = (lax.broadcasted_iota(jnp.int32, (rows, rows), 0)
           >= lax.broadcasted_iota(jnp.int32, (rows, rows), 1)).astype(BF16)
    c3 = jnp.dot(tri, _split3_bf16(log_f), preferred_element_type=F32)
    cum = c3[:, 0:LANES] + c3[:, LANES:2 * LANES] + c3[:, 2 * LANES:3 * LANES] + carry_ref[...]
    carry_ref[...] = cum[rows - 1:rows, :]
    fkb = jnp.dot(_split3_bf16(cum * (-LOG2E)), e_ref[...], preferred_element_type=F32)
    fk = proj[:, _C_FK:_C_FV] + fkb
    for h in range(FOX_HEADS):
        fq_ref[0, h] = fq[:, 128 * h:128 * h + 128].astype(BF16)
        fk_ref[0, h] = fk[:, 128 * h:128 * h + 128].astype(BF16)
    for p in range(FOX_HEADS // 2):
        fvt_ref[0, p, 0] = proj[:, _C_FV + 128 * p:_C_FV + 128 * p + 128].T.astype(BF16)


def _prep_call(h, tables, w1, wuq, wukv, gq, gkv, emat, ones, b_ff, batch, seq, q_scales, ln_in=None):
    tm = ROW_TILE
    nt = seq // tm
    c128, s128, c256, s256 = tables
    row = lambda w: pl.BlockSpec((tm, w), lambda i: (i, 0))
    full = lambda a: pl.BlockSpec(a.shape, lambda i: (0,) * a.ndim)
    head = lambda nh: pl.BlockSpec((1, nh, tm, 128), lambda i: (i // nt, 0, i % nt, 0))
    vt = lambda npair: pl.BlockSpec((1, npair, 1, 128, tm), lambda i: (i // nt, 0, i % nt, 0, 0))
    hshape = lambda nh: jax.ShapeDtypeStruct((batch, nh, seq, 128), BF16)
    vshape = lambda npair: jax.ShapeDtypeStruct((batch, npair, nt, 128, tm), BF16)
    norm_input = ln_in is not None
    ln_args = list(ln_in) if norm_input else []
    weights = [w1, wuq, wukv, gq, gkv, emat, ones, b_ff]
    return pl.pallas_call(
        functools.partial(_prep_kernel, q_scales, nt, norm_input), grid=(batch * nt,),
        in_specs=([row(D_MODEL)] + [full(a) for a in ln_args] + [row(128), row(128), row(256), row(256)]
                  + [full(a) for a in weights]),
        out_specs=([row(D_MODEL)] if norm_input else []) + [
            head(8), head(8), vt(4), head(2), head(2), vt(2), head(4), head(4), vt(2)],
        out_shape=([jax.ShapeDtypeStruct(h.shape, F32)] if norm_input else []) + [
            hshape(8), hshape(8), vshape(4), hshape(2), hshape(2), vshape(2), hshape(4), hshape(4), vshape(2)],
        scratch_shapes=[pltpu.VMEM((1, LANES), F32)],
        compiler_params=_cparams(("arbitrary",)), name="prep")(
            h, *ln_args, c128, s128, c256, s256, *weights)


def _attn_kernel(kind, lam_init, *refs):
    if kind == "diff":
        lam_ref, q_ref, k_ref, vt_ref, g_ref, o_ref, m_ref, acc_ref, s_ref, bm_ref = refs
    else:
        q_ref, k_ref, vt_ref, o_ref, m_ref, acc_ref, s_ref, bm_ref = refs
    t = ATTN_TILE
    qi = pl.program_id(1)
    npair = vt_ref.shape[1]

    qs, k_of_map, v_of_map = [], [], []
    if kind == "diff":
        for p in range(npair):
            q2 = q_ref[0, p]
            lane = lax.broadcasted_iota(jnp.int32, q2.shape, 1)
            for sub in range(4):
                keep = (lane >= DIFF_QK * sub) & (lane < DIFF_QK * (sub + 1))
                qs.append(jnp.where(keep, q2, jnp.zeros_like(q2)))
                k_of_map.append(p)
                v_of_map.append((p, sub // 2))
    else:
        for h in range(2 * npair):
            qs.append(q_ref[0, h])
            k_of_map.append(h)
            v_of_map.append((h // 2, h % 2))
    nmap = len(qs)

    m_ref[...] = jnp.full(m_ref.shape, NEG_INF, F32)
    acc_ref[...] = jnp.zeros(acc_ref.shape, F32)

    kk = lax.broadcasted_iota(jnp.int32, (t, t), 0)
    qq = lax.broadcasted_iota(jnp.int32, (t, t), 1)
    if kind == "fox":
        diag_mask = kk <= qq
    else:
        diag_mask = (kk // CHUNK) <= (qq // CHUNK)
    ones_rows = jnp.ones((SUM_ROWS, t), BF16)

    def score_block(j, slot, maps=None):
        start = pl.multiple_of(j * t, t)
        for m in (range(nmap) if maps is None else maps):
            kblk = k_ref[0, k_of_map[m], pl.ds(start, t), :]
            s = lax.dot_general(kblk, qs[m], (((1,), (1,)), ((), ())), preferred_element_type=F32)
            s_ref[slot, m] = s
            bm_ref[slot, m] = jnp.max(s, axis=0, keepdims=True)

    def softmax_block(j, slot, masked, maps=None):
        maps = list(range(nmap)) if maps is None else list(maps)
        probs, alphas = {}, {}
        for m in maps:
            s = s_ref[slot, m]
            m_old = m_ref[m]
            if masked:
                s = jnp.where(diag_mask, s, NEG_INF)
                m_new = jnp.maximum(m_old, jnp.max(s, axis=0, keepdims=True))
            else:
                m_new = jnp.maximum(m_old, bm_ref[slot, m])
            alphas[m] = jnp.exp2(m_old - m_new)
            probs[m] = jnp.exp2(s - m_new).astype(BF16)
            m_ref[m] = m_new
        for m in maps:
            pair, hd = v_of_map[m]
            vblk = jnp.concatenate([vt_ref[0, pair, j, HEAD_DIM * hd:HEAD_DIM * (hd + 1), :], ones_rows],
                                   axis=0)
            pv = jnp.dot(vblk, probs[m], preferred_element_type=F32)
            acc_ref[m] = alphas[m] * acc_ref[m] + pv

    score_block(0, 0)
    unroll = KV_TILES_PER_STEP
    group = MAPS_PER_GROUP[kind]

    def pipelined_tile(j, u):
        for g in range(0, nmap, group):
            part = range(g, g + group)
            score_block(j + 1, (u + 1) % 2, part)
            softmax_block(j, u % 2, False, part)

    def tiles(jj, carry):
        j = unroll * jj
        for u in range(unroll):
            pipelined_tile(j + u, u)
        return carry

    lax.fori_loop(0, qi // unroll, tiles, 0)
    base = qi - qi % unroll
    for rem in range(unroll):
        @pl.when(qi % unroll == rem)
        def _():
            for u in range(rem):
                pipelined_tile(base + u, u)
            softmax_block(qi, rem % 2, True)

    def normalized(m):
        return acc_ref[m, 0:HEAD_DIM, :] / acc_ref[m, HEAD_DIM:HEAD_DIM + 1, :]

    for p in range(npair):
        if kind == "diff":
            lam = lam_ref[0]
            outs = []
            for hd in range(2):
                o = normalized(4 * p + 2 * hd) - lam * normalized(4 * p + 2 * hd + 1)
                outs.append(o * lax.rsqrt(jnp.mean(o * o, axis=0, keepdims=True) + RMS_EPS))
            ot = jnp.concatenate(outs, axis=0).T
            ot = ot * g_ref[...] * (1.0 - lam_init)
        else:
            ot = jnp.concatenate([normalized(2 * p), normalized(2 * p + 1)], axis=0).T
        o_ref[:, 128 * p:128 * (p + 1)] = ot.astype(BF16)


def _attn_call(kind, q, k, vt, batch, seq, lam=None, lam_init=0.0, subln_g=None):
    t = ATTN_TILE
    nq = seq // t
    npair = vt.shape[1]
    nslab = q.shape[1]
    nmap = 4 * npair if kind == "diff" else 2 * npair
    resident = pl.Buffered(1)
    in_specs = [pl.BlockSpec((1, nslab, t, 128), lambda b, i: (b, 0, i, 0)),
                pl.BlockSpec((1, nslab, seq, 128), lambda b, i: (b, 0, 0, 0), pipeline_mode=resident),
                pl.BlockSpec((1, npair, nq, 128, t), lambda b, i: (b, 0, 0, 0, 0), pipeline_mode=resident)]
    args = [q, k, vt]
    if kind == "diff":
        in_specs = [pl.BlockSpec(memory_space=pltpu.SMEM)] + in_specs + [
            pl.BlockSpec((1, 128), lambda b, i: (0, 0))]
        args = [lam] + args + [subln_g]
    return pl.pallas_call(
        functools.partial(_attn_kernel, kind, lam_init), grid=(batch, nq),
        in_specs=in_specs,
        out_specs=pl.BlockSpec((t, 128 * npair), lambda b, i: (b * nq + i, 0)),
        out_shape=jax.ShapeDtypeStruct((batch * seq, 128 * npair), BF16),
        scratch_shapes=[pltpu.VMEM((nmap, 1, t), F32),
                        pltpu.VMEM((nmap, HEAD_DIM + SUM_ROWS, t), F32),
                        pltpu.VMEM((2, nmap, t, t), F32), pltpu.VMEM((2, nmap, 1, t), F32)],
        compiler_params=_cparams(("parallel", "arbitrary")),
        name="attn_" + kind)(*args)


def _out_kernel(alpha, om_ref, od_ref, of_ref, h_ref, wo_ref, g_ref, b_ref, rhi_ref, rlo_ref, rb_ref,
                h1_ref, meta_ref, cnt_out_ref, cnt_ref, sel_ref):
    @pl.when(pl.program_id(0) == 0)
    def _():
        cnt_ref[...] = jnp.zeros_like(cnt_ref)

    mix = (jnp.dot(om_ref[...], wo_ref[0:512, :], preferred_element_type=F32)
           + jnp.dot(od_ref[...], wo_ref[512:768, :], preferred_element_type=F32)
           + jnp.dot(of_ref[...], wo_ref[768:1024, :], preferred_element_type=F32))
    h1 = _layer_norm(alpha * h_ref[...] + mix, g_ref[...], b_ref[...])
    h1_ref[...] = h1

    h_hi, h_lo = _split_bf16(h1)
    nt = lambda x, y: lax.dot_general(x, y, (((1,), (1,)), ((), ())), preferred_element_type=F32)
    logits = nt(rhi_ref[...], h_hi) + nt(rhi_ref[...], h_lo) + nt(rlo_ref[...], h_hi)
    score = jax.nn.sigmoid(logits)
    biased = score + rb_ref[...]
    b = [biased[e:e + 1, :] for e in range(N_EXPERTS)]
    sc = [score[e:e + 1, :] for e in range(N_EXPERTS)]

    gscore = []
    for g in range(N_GROUPS):
        x = b[GROUP_SIZE * g:GROUP_SIZE * (g + 1)]
        pair_sums = [x[i] + x[j] for i in range(GROUP_SIZE) for j in range(i + 1, GROUP_SIZE)]
        gscore.append(functools.reduce(jnp.maximum, pair_sums))
    sel = []
    for g in range(N_GROUPS):
        first_max = None
        for g2 in range(N_GROUPS):
            if g2 == g:
                continue
            c = (gscore[g] > gscore[g2]) if g2 < g else (gscore[g] >= gscore[g2])
            first_max = c if first_max is None else (first_max & c)
        x = b[GROUP_SIZE * g:GROUP_SIZE * (g + 1)]
        for j in range(GROUP_SIZE):
            ahead = jnp.zeros_like(x[j])
            for i in range(GROUP_SIZE):
                if i < j:
                    ahead = ahead + (x[i] >= x[j]).astype(F32)
                elif i > j:
                    ahead = ahead + (x[i] > x[j]).astype(F32)
            sel.append(first_max & (ahead < 2.0))

    zero = jnp.zeros_like(sc[0])
    denom = zero
    e_lo = jnp.full_like(zero, 99.0)
    e_hi = jnp.full_like(zero, -1.0)
    for e in range(N_EXPERTS):
        sel_ref[e:e + 1, :] = sel[e].astype(F32)
        denom = denom + jnp.where(sel[e], sc[e], 0.0)
        e_lo = jnp.minimum(e_lo, jnp.where(sel[e], float(e), 99.0))
        e_hi = jnp.maximum(e_hi, jnp.where(sel[e], float(e), -1.0))

    selm = sel_ref[...]
    rows = selm.shape[1]
    r = lax.broadcasted_iota(jnp.int32, (rows, rows), 0)
    c = lax.broadcasted_iota(jnp.int32, (rows, rows), 1)
    before = (r < c).astype(BF16)
    rank = jnp.dot(selm.astype(BF16), before, preferred_element_type=F32) + cnt_ref[:, 0:1]
    g_lo, g_hi, r_lo, r_hi = zero, zero, zero, zero
    for e in range(N_EXPERTS):
        gate = sc[e] / denom
        is_lo = e_lo == float(e)
        is_hi = e_hi == float(e)
        g_lo = g_lo + jnp.where(is_lo, gate, 0.0)
        g_hi = g_hi + jnp.where(is_hi, gate, 0.0)
        r_lo = r_lo + jnp.where(is_lo, rank[e:e + 1, :], 0.0)
        r_hi = r_hi + jnp.where(is_hi, rank[e:e + 1, :], 0.0)
    for i, v in enumerate((e_lo, e_hi, r_lo, r_hi, g_lo, g_hi, zero, zero)):
        meta_ref[i:i + 1, :] = v
    cnt_ref[...] = cnt_ref[...] + jnp.sum(selm, axis=1, keepdims=True)
    cnt_out_ref[...] = cnt_ref[...]


def _out_call(om, od, of, h, wo, g, b, rwt, rb, alpha):
    r_hi, r_lo = _split_bf16(rwt)
    n = h.shape[0]
    tm = OUT_TILE
    row = lambda w: pl.BlockSpec((tm, w), lambda i: (i, 0))
    full = lambda a: pl.BlockSpec(a.shape, lambda i: (0,) * a.ndim)
    return pl.pallas_call(
        functools.partial(_out_kernel, alpha), grid=(n // tm,),
        in_specs=[row(512), row(256), row(256), row(D_MODEL), full(wo), full(g), full(b),
                  full(r_hi), full(r_lo), full(rb)],
        out_specs=[row(D_MODEL), pl.BlockSpec((SUBLANES, tm), lambda i: (0, i)),
                   pl.BlockSpec((N_EXPERTS, LANES), lambda i: (0, 0))],
        out_shape=[jax.ShapeDtypeStruct((n, D_MODEL), F32), jax.ShapeDtypeStruct((SUBLANES, n), F32),
                   jax.ShapeDtypeStruct((N_EXPERTS, LANES), F32)],
        scratch_shapes=[pltpu.VMEM((N_EXPERTS, LANES), F32), pltpu.VMEM((N_EXPERTS, tm), F32)],
        compiler_params=_cparams(("arbitrary",)), name="outproj_router")(
            om, od, of, h, wo, g, b, r_hi, r_lo, rb)


def _row_copy(src_ref, src_row, dst_ref, dst_row, sem):
    return pltpu.make_async_copy(src_ref.at[pl.ds(src_row, 1)], dst_ref.at[pl.ds(dst_row, 1)], sem)


def _dispatch_kernel(fill_ref, rows_ref, h_ref, xs_ref, zero_ref, sem):
    tm = h_ref.shape[0]

    @pl.when(pl.program_id(0) == 0)
    def _():
        zero_ref[...] = jnp.zeros_like(zero_ref)
        n_fixed = fill_ref.shape[0] - 1
        tail = fill_ref[n_fixed]
        n_tail = (xs_ref.shape[0] - tail) // tm

        def chunk_copy(row):
            return pltpu.make_async_copy(zero_ref, xs_ref.at[pl.ds(pl.multiple_of(row, tm), tm)], sem)

        def chunk_row(c):
            return jnp.where(c < n_fixed, fill_ref[jnp.minimum(c, n_fixed - 1)], tail + (c - n_fixed) * tm)

        def fill(c, carry):
            row = chunk_row(c)

            @pl.when(row >= 0)
            def _():
                chunk_copy(row).start()
            return carry

        def fill_wait(c, carry):
            row = chunk_row(c)

            @pl.when(row >= 0)
            def _():
                chunk_copy(row).wait()
            return carry

        lax.fori_loop(0, n_fixed + n_tail, fill, 0)
        lax.fori_loop(0, n_fixed + n_tail, fill_wait, 0)

    def start(c, carry):
        for u in range(DMA_UNROLL):
            r = c * DMA_UNROLL + u
            for s in range(2):
                _row_copy(h_ref, r, xs_ref, rows_ref[0, 0, 2 * r + s], sem).start(priority=s)
        return carry

    def wait(c, carry):
        for _ in range(2 * DMA_UNROLL):
            _row_copy(h_ref, 0, xs_ref, 0, sem).wait()
        return carry

    lax.fori_loop(0, tm // DMA_UNROLL, start, 0)
    lax.fori_loop(0, tm // DMA_UNROLL, wait, 0)


def _dispatch_call(fill, rows, h1, total_rows):
    n, d = h1.shape
    tm = ROW_TILE
    grid_spec = pltpu.PrefetchScalarGridSpec(
        num_scalar_prefetch=1, grid=(n // tm,),
        in_specs=[pl.BlockSpec((1, 1, 2 * tm), lambda i, f: (i, 0, 0), memory_space=pltpu.SMEM),
                  pl.BlockSpec((tm, d), lambda i, f: (i, 0))],
        out_specs=pl.BlockSpec(memory_space=pl.ANY),
        scratch_shapes=[pltpu.VMEM((tm, d), F32), pltpu.SemaphoreType.DMA(())])
    return pl.pallas_call(
        _dispatch_kernel, grid_spec=grid_spec,
        out_shape=jax.ShapeDtypeStruct((total_rows, d), F32),
        compiler_params=_cparams(("arbitrary",)), name="moe_dispatch")(fill, rows, h1)


def _moe_kernel(te_ref, nu_ref, x_ref, wg_ref, wu_ref, wd_ref, y_ref):
    del te_ref
    i = pl.program_id(0)

    @pl.when(i < nu_ref[0])
    def _():
        x = x_ref[...].astype(BF16)
        gate = jnp.dot(x, wg_ref[0], preferred_element_type=F32)
        up = jnp.dot(x, wu_ref[0], preferred_element_type=F32)
        a = (gate * jax.nn.sigmoid(gate) * up).astype(BF16)
        y_ref[...] = jnp.dot(a, wd_ref[0], preferred_element_type=F32)

    @pl.when(i >= nu_ref[0])
    def _():
        y_ref[...] = jnp.zeros_like(y_ref)


def _moe_call(tile_expert, n_used, xs, wg, wu, wd):
    total_rows, d = xs.shape
    tm = MOE_TILE
    grid_spec = pltpu.PrefetchScalarGridSpec(
        num_scalar_prefetch=2, grid=(total_rows // tm,),
        in_specs=[pl.BlockSpec((tm, d), lambda i, te, nu: (jnp.minimum(i, nu[0] - 1), 0)),
                  pl.BlockSpec((1, d, D_EXPERT), lambda i, te, nu: (te[i], 0, 0)),
                  pl.BlockSpec((1, d, D_EXPERT), lambda i, te, nu: (te[i], 0, 0)),
                  pl.BlockSpec((1, D_EXPERT, d), lambda i, te, nu: (te[i], 0, 0))],
        out_specs=pl.BlockSpec((tm, d), lambda i, te, nu: (i, 0)))
    return pl.pallas_call(
        _moe_kernel, grid_spec=grid_spec,
        out_shape=jax.ShapeDtypeStruct((total_rows, d), F32),
        compiler_params=_cparams(("arbitrary",)), name="moe_experts")(
            tile_expert, n_used, xs, wg, wu, wd)


def _combine_kernel(alpha, rows_ref, next_rows_ref, meta_ref, h1_ref, ys_ref, g_ref, b_ref, o_ref, ybuf_ref, sem):
    tm = h1_ref.shape[0]
    i = pl.program_id(0)
    slot = i % 2

    def gather(table_ref, into):
        def start(c, carry):
            for u in range(DMA_UNROLL):
                r = c * DMA_UNROLL + u
                for s in range(2):
                    _row_copy(ys_ref, table_ref[0, 0, 2 * r + s], ybuf_ref.at[into, s], r,
                              sem.at[into]).start(priority=s)
            return carry
        lax.fori_loop(0, tm // DMA_UNROLL, start, 0)

    @pl.when(i == 0)
    def _():
        gather(rows_ref, 0)

    @pl.when(i + 1 < pl.num_programs(0))
    def _():
        gather(next_rows_ref, 1 - slot)

    meta_rows = jnp.concatenate([meta_ref[...], jnp.zeros((LANES - SUBLANES, tm), F32)], axis=0)
    gcol = meta_rows.T

    def wait(c, carry):
        for _ in range(2 * DMA_UNROLL):
            _row_copy(ys_ref, 0, ybuf_ref.at[slot, 0], 0, sem.at[slot]).wait()
        return carry

    lax.fori_loop(0, tm // DMA_UNROLL, wait, 0)
    ffn = gcol[:, 4:5] * ybuf_ref[slot, 0] + gcol[:, 5:6] * ybuf_ref[slot, 1]
    o_ref[...] = _layer_norm(alpha * h1_ref[...] + ffn, g_ref[...], b_ref[...])


def _combine_call(rows, meta, h1, ys, g, b, alpha):
    n, d = h1.shape
    tm = ROW_TILE
    vec = pl.BlockSpec((1, d), lambda i: (0, 0))
    last = n // tm - 1
    return pl.pallas_call(
        functools.partial(_combine_kernel, alpha), grid=(n // tm,),
        in_specs=[pl.BlockSpec((1, 1, 2 * tm), lambda i: (i, 0, 0), memory_space=pltpu.SMEM),
                  pl.BlockSpec((1, 1, 2 * tm), lambda i: (jnp.minimum(i + 1, last), 0, 0),
                               memory_space=pltpu.SMEM),
                  pl.BlockSpec((SUBLANES, tm), lambda i: (0, i)),
                  pl.BlockSpec((tm, d), lambda i: (i, 0)),
                  pl.BlockSpec(memory_space=pl.ANY), vec, vec],
        out_specs=pl.BlockSpec((tm, d), lambda i: (i, 0)),
        out_shape=jax.ShapeDtypeStruct((n, d), F32),
        scratch_shapes=[pltpu.VMEM((2, 2, tm, d), F32), pltpu.SemaphoreType.DMA((2,))],
        compiler_params=_cparams(("arbitrary",)), name="moe_combine")(rows, rows, meta, h1, ys, g, b)


def _widen_w_in(w_in):
    z = lambda c: jnp.zeros((D_MODEL, c), F32)
    c_q, c_kv = w_in[:, 0:384], w_in[:, 384:640]
    kr = w_in[:, 640:672]
    dq, dk, dv = w_in[:, 672:928], w_in[:, 928:1184], w_in[:, 1184:1440]
    fq, fk, fv = w_in[:, 1440:1696], w_in[:, 1696:1952], w_in[:, 1952:2208]
    kr_blk = jnp.concatenate([z(64), kr, z(32)], axis=1)
    kr_swp = jnp.concatenate([z(64), -kr[:, 16:32], kr[:, 0:16], z(32)], axis=1)

    def diff_swap(w):
        w4 = w.reshape(D_MODEL, 8, DIFF_QK)
        half = DIFF_ROT // 2
        sw = jnp.concatenate([-w4[:, :, half:DIFF_ROT], w4[:, :, 0:half],
                              jnp.zeros((D_MODEL, 8, DIFF_QK - DIFF_ROT), F32)], axis=2)
        return sw.reshape(D_MODEL, 256)

    def pad_heads(w):
        w4 = w.reshape(D_MODEL, FOX_HEADS, HEAD_DIM)
        return jnp.concatenate([w4, jnp.zeros_like(w4)], axis=2).reshape(D_MODEL, FOX_HEADS * 128)

    w_ff = jnp.concatenate([w_in[:, 2208:2212], z(LANES - FOX_HEADS)], axis=1)
    w1 = jnp.concatenate([c_q, c_kv, kr_blk, kr_swp, dq, diff_swap(dq), dk, diff_swap(dk), dv,
                          pad_heads(fq), pad_heads(fk), fv, w_ff], axis=1)
    assert w1.shape[1] == _C_END
    return w1.astype(BF16)


def _widen_mla(w_uq, w_ukv):
    wq = w_uq.reshape(MLA_Q_RANK, MLA_HEADS, MLA_NOPE + MLA_ROPE)
    zq = jnp.zeros((MLA_Q_RANK, MLA_HEADS, 32), F32)
    half = MLA_ROPE // 2
    q_main = jnp.concatenate([wq, zq], axis=2)
    q_swap = jnp.concatenate([jnp.zeros((MLA_Q_RANK, MLA_HEADS, MLA_NOPE), F32),
                              -wq[:, :, MLA_NOPE + half:], wq[:, :, MLA_NOPE:MLA_NOPE + half], zq], axis=2)
    wuq = jnp.concatenate([q_main.reshape(MLA_Q_RANK, 1024), q_swap.reshape(MLA_Q_RANK, 1024)], axis=1)
    wkv = w_ukv.reshape(MLA_KV_RANK, MLA_HEADS, 128)
    k_part = jnp.concatenate([wkv[:, :, :MLA_NOPE], jnp.zeros((MLA_KV_RANK, MLA_HEADS, 64), F32)], axis=2)
    v_part = wkv[:, :, MLA_NOPE:]
    wukv = jnp.concatenate([k_part.reshape(MLA_KV_RANK, 1024), v_part.reshape(MLA_KV_RANK, 512)], axis=1)
    return wuq.astype(BF16), wukv.astype(BF16)


def _fox_lane_constants():
    ones = np.zeros((1, FOX_HEADS * 128), np.float32)
    emat = np.zeros((3 * LANES, FOX_HEADS * 128), np.float32)
    for h in range(FOX_HEADS):
        for part in range(3):
            ones[0, 128 * h + HEAD_DIM + part] = 1.0
            emat[LANES * part + h, 128 * h + HEAD_DIM + part] = 1.0
    return jnp.asarray(ones), jnp.asarray(emat, dtype=BF16)


def _rope_lane_constants():
    n_m, n_d = MLA_ROPE // 2, DIFF_ROT // 2
    inv_m = ROPE_THETA ** (-jnp.arange(0, MLA_ROPE, 2, dtype=F32) / MLA_ROPE)
    inv_d = ROPE_THETA ** (-jnp.arange(0, DIFF_ROT, 2, dtype=F32) / DIFF_ROT)
    freq = jnp.concatenate([inv_m, inv_d, jnp.zeros((LANES - n_m - n_d,), F32)]).reshape(1, LANES)
    src128 = np.full((128,), LANES - 1)
    src128[MLA_NOPE:MLA_NOPE + n_m] = np.arange(n_m)
    src128[MLA_NOPE + n_m:MLA_NOPE + 2 * n_m] = np.arange(n_m)
    src32 = np.full((DIFF_QK,), LANES - 1)
    src32[0:n_d] = n_m + np.arange(n_d)
    src32[n_d:2 * n_d] = n_m + np.arange(n_d)
    src256 = np.tile(src32, 8)

    def copy_matrix(src):
        e = np.zeros((LANES, src.shape[0]), np.float32)
        e[src, np.arange(src.shape[0])] = 1.0
        return jnp.asarray(np.concatenate([e, e, e], axis=0), dtype=BF16)

    return freq, copy_matrix(src128), copy_matrix(src256)


def kernel(x, positions, ln_in_g, ln_in_b, w_in, mla_q_norm_g, mla_kv_norm_g, mla_w_uq, mla_w_ukv,
           diff_lam_q1, diff_lam_k1, diff_lam_q2, diff_lam_k2, diff_subln_g, fox_f_bias, w_out,
           ln1_g, ln1_b, router_w, router_bias, exp_w_gate, exp_w_up, exp_w_down, ln2_g, ln2_b):
    batch, seq, d = x.shape
    depth = w_in.shape[0]
    n = batch * seq
    alpha = (2 * depth) ** 0.25
    total_rows = 2 * n + N_EXPERTS * MOE_TILE
    n_tiles = total_rows // MOE_TILE
    q_scales = ((MLA_NOPE + MLA_ROPE) ** -0.5 * LOG2E, DIFF_QK ** -0.5 * LOG2E, HEAD_DIM ** -0.5 * LOG2E)

    row = lambda v: v.reshape(1, -1).astype(F32)
    tables = _rope_call(positions.reshape(n, 1).astype(jnp.int32), *_rope_lane_constants())
    ones_fox, emat = _fox_lane_constants()
    rwt = router_w.T.astype(F32)
    rb = router_bias.reshape(N_EXPERTS, 1).astype(F32)

    h = x.reshape(n, d).astype(F32)
    for l in range(depth):
        lam_init = 0.8 - 0.6 * math.exp(-0.3 * l)
        lam = (jnp.exp(jnp.sum(diff_lam_q1[l].astype(F32) * diff_lam_k1[l].astype(F32)))
               - jnp.exp(jnp.sum(diff_lam_q2[l].astype(F32) * diff_lam_k2[l].astype(F32))) + lam_init)
        w1 = _widen_w_in(w_in[l])
        wuq, wukv = _widen_mla(mla_w_uq[l], mla_w_ukv[l])
        b_ff = jnp.concatenate([fox_f_bias[l].astype(F32), jnp.zeros((LANES - FOX_HEADS,), F32)]).reshape(1, LANES)

        prep_out = _prep_call(h, tables, w1, wuq, wukv, row(mla_q_norm_g[l]), row(mla_kv_norm_g[l]),
                              emat, ones_fox, b_ff, batch, seq, q_scales,
                              ln_in=(row(ln_in_g), row(ln_in_b)) if l == 0 else None)
        if l == 0:
            h, *prep_out = prep_out
        mq, mk, mvt, dq, dk, dvt, fq, fk, fvt = prep_out
        o_mla = _attn_call("mla", mq, mk, mvt, batch, seq)
        o_diff = _attn_call("diff", dq, dk, dvt, batch, seq, lam=lam.reshape(1).astype(F32),
                            lam_init=lam_init, subln_g=row(jnp.tile(diff_subln_g[l], 2)))
        o_fox = _attn_call("fox", fq, fk, fvt, batch, seq)

        h1, meta, cnt = _out_call(o_mla, o_diff, o_fox, h, w_out[l].astype(BF16), row(ln1_g[l]),
                                  row(ln1_b[l]), rwt, rb, alpha)

        counts = cnt[:, 0].astype(jnp.int32)
        padded = ((counts + MOE_TILE - 1) // MOE_TILE) * MOE_TILE
        ends = jnp.cumsum(padded)
        starts = ends - padded
        e_idx = meta[0:2].astype(jnp.int32)
        expert_ids = jnp.arange(N_EXPERTS, dtype=jnp.int32)[:, None, None]
        rows = jnp.sum(jnp.where(e_idx[None] == expert_ids, starts[:, None, None], 0), axis=0)
        rows = rows + meta[2:4].astype(jnp.int32)
        rows = rows.T.reshape(n // ROW_TILE, 1, 2 * ROW_TILE)
        tile_start = jnp.arange(n_tiles, dtype=jnp.int32) * MOE_TILE
        tile_expert = jnp.minimum(jnp.sum(tile_start[:, None] >= ends[None, :], axis=1),
                                  N_EXPERTS - 1).astype(jnp.int32)
        n_used = (ends[-1:] // MOE_TILE).astype(jnp.int32)

        last_tile = jnp.where(padded > 0, ends - MOE_TILE, -MOE_TILE)
        fill = jnp.concatenate([(last_tile[:, None] + jnp.arange(0, MOE_TILE, ROW_TILE)[None, :]).reshape(-1),
                                ends[-1:]]).astype(jnp.int32)
        xs = _dispatch_call(fill, rows, h1, total_rows)
        ys = _moe_call(tile_expert, n_used, xs, exp_w_gate[l].astype(BF16), exp_w_up[l].astype(BF16),
                       exp_w_down[l].astype(BF16))
        h = _combine_call(rows, meta, h1, ys, row(ln2_g[l]), row(ln2_b[l]), alpha)
    return h.reshape(batch, seq, d)
```
